```python
import math
import jax, jax.numpy as jnp
from jax import lax
import numpy as np

D_MODEL = 1024
BATCH = 4
SEQ = 8192
DEPTH = 2

GRID_W = 64
CTX_LEN = 256
N_EVEN = (DEPTH + 1) // 2
N_ODD = DEPTH // 2
EPS = 1e-6

M_HEADS = 4
M_DK = 128
M_DV = 128
M_CHUNK = 64
B_HEADS = 8
B_Q_RANK = 256
B_KV_RANK = 128
B_NOPE = 64
B_ROPE = 32
B_V = 64
ROPE_BASE = 10000.0
Q_BLOCK = 128
C_HEADS = 16
C_HEAD_DIM = 64
NA_KH_MAX = 8
NA_KW = 16
D_FF = 4 * D_MODEL

M_WIDTH = M_HEADS * M_DV
B_WIDTH = B_HEADS * B_V
AB_WIDTH = M_WIDTH + B_WIDTH
C_WIDTH = C_HEADS * C_HEAD_DIM
AB_SPLITS = (M_HEADS * M_DK, M_HEADS * M_DK, M_HEADS * M_DV, M_HEADS * M_DV, 4 * M_HEADS,
             B_Q_RANK, B_KV_RANK, B_ROPE)
AB_IN = sum(AB_SPLITS)

kernel_name = "hybrid_mlstm_mla_natten_dit_block"


def rmsnorm(x, g):
    xf = x.astype(jnp.float32)
    y = xf * lax.rsqrt(jnp.mean(xf * xf, axis=-1, keepdims=True) + EPS)
    return y.astype(x.dtype) * g


def modulate(x, g, shift, scale):
    return rmsnorm(x, g) * (1 + scale) + shift


def split_cols(a, sizes):
    idx = np.cumsum(np.array(sizes))[:-1].tolist()
    return jnp.split(a, idx, axis=-1)


def sqrelu_mlp(h, w1, w2):
    return jnp.square(jax.nn.relu(h @ w1)) @ w2


def rope_2d(T):
    pos = jnp.arange(T)
    row = (pos // GRID_W).astype(jnp.float32)
    col = (pos % GRID_W).astype(jnp.float32)
    n_f = B_ROPE // 4
    freqs = ROPE_BASE ** (-jnp.arange(n_f, dtype=jnp.float32) / n_f)
    ang = jnp.concatenate([row[:, None] * freqs, col[:, None] * freqs], axis=-1)
    return jnp.cos(ang)[:, None, :], jnp.sin(ang)[:, None, :]


def apply_rope(x, cos, sin):
    half = x.shape[-1] // 2
    xa = x[..., :half].astype(jnp.float32)
    xb = x[..., half:].astype(jnp.float32)
    out = jnp.concatenate([xa * cos - xb * sin, xa * sin + xb * cos], axis=-1)
    return out.astype(x.dtype)


def dense_attention(q, k, v, scale):
    s = jnp.einsum('bqhd,bkhd->bhqk', q, k).astype(jnp.float32) * scale
    p = jax.nn.softmax(s, axis=-1).astype(v.dtype)
    return jnp.einsum('bhqk,bkhd->bqhd', p, v)


def blocked_attention(q, k, v, scale):
    B, T, H, dq = q.shape
    nb = T // Q_BLOCK
    qb = jnp.moveaxis(q.reshape(B, nb, Q_BLOCK, H, dq), 1, 0)
    o = lax.map(lambda qi: dense_attention(qi, k, v, scale), qb)
    return jnp.moveaxis(o, 0, 1).reshape(B, T, H, v.shape[-1])


def mlstm_zero_state(B):
    return (jnp.zeros((B, M_HEADS, M_DK, M_DV), jnp.float32),
            jnp.zeros((B, M_HEADS, M_DK), jnp.float32),
            jnp.zeros((B, M_HEADS), jnp.float32))


def mlstm_chunked(q, k, v, log_i, log_f, state):
    B, H, T, _ = q.shape
    L = M_CHUNK
    nc = T // L

    def chunks(a):
        return jnp.moveaxis(a.reshape(a.shape[:2] + (nc, L) + a.shape[3:]), 2, 0)

    tril = jnp.tril(jnp.ones((L, L), dtype=bool))

    def step(carry, inp):
        C, n, m = carry
        qc, kc, vc, li, lf = inp
        b = jnp.cumsum(lf, axis=-1)
        d_mat = jnp.where(tril, b[..., :, None] - b[..., None, :] + li[..., None, :], -jnp.inf)
        inter = b + m[..., None]
        m_t = jnp.maximum(jnp.max(d_mat, axis=-1), inter)
        s = jnp.einsum('bhtd,bhsd->bhts', qc, kc) * jnp.exp(d_mat - m_t[..., None])
        w_inter = jnp.exp(inter - m_t)
        num = (w_inter[..., None] * jnp.einsum('bhtd,bhde->bhte', qc, C)
               + jnp.einsum('bhts,bhse->bhte', s, vc))
        den = w_inter * jnp.einsum('bhtd,bhd->bht', qc, n) + jnp.sum(s, axis=-1)
        h = num / jnp.maximum(jnp.abs(den), jnp.exp(-m_t))[..., None]
        b_last = b[..., -1]
        g = b_last[..., None] - b + li
        m_new = jnp.maximum(b_last + m, jnp.max(g, axis=-1))
        w_k = jnp.exp(g - m_new[..., None])
        decay = jnp.exp(b_last + m - m_new)
        C_new = decay[..., None, None] * C + jnp.einsum('bhs,bhsd,bhse->bhde', w_k, kc, vc)
        n_new = decay[..., None] * n + jnp.einsum('bhs,bhsd->bhd', w_k, kc)
        return (C_new, n_new, m_new), h

    state, hs = lax.scan(step, state, (chunks(q), chunks(k), chunks(v), chunks(log_i), chunks(log_f)))
    h = jnp.moveaxis(hs, 0, 2).reshape(B, H, T, v.shape[-1])
    return h, state


def mlstm_bidir(q, k, v, gates, states):
    i_f, f_f, i_b, f_b = gates
    h_f, s_f = mlstm_chunked(q, k, v, i_f, f_f, states[0])
    flip = lambda a: jnp.flip(a, axis=2)
    h_b, s_b = mlstm_chunked(flip(q), flip(k), flip(v), flip(i_b), flip(f_b), states[1])
    return h_f + flip(h_b), (s_f, s_b)


def mlstm_heads(a, d):
    B, T, _ = a.shape
    return a.reshape(B, T, M_HEADS, d).transpose(0, 2, 1, 3).astype(jnp.float32)


def mlstm_gates(pre, gate_b):
    B, T, _ = pre.shape
    g = (pre + gate_b).astype(jnp.float32).reshape(B, T, 4, M_HEADS).transpose(2, 0, 3, 1)
    return (g[0], jax.nn.log_sigmoid(g[1]), g[2], jax.nn.log_sigmoid(g[3]))


def mlstm_out(hm, o, norm_g):
    B, H, T, dv = hm.shape
    hn = rmsnorm(hm, norm_g[:, None, :]).transpose(0, 2, 1, 3).reshape(B, T, H * dv)
    return (hn * jax.nn.sigmoid(o.astype(jnp.float32))).astype(o.dtype)


def mla_qkv(cq, ckv, kr, q_norm_g, kv_norm_g, w_uq, w_ukv, rope):
    B, T, _ = cq.shape
    q = (rmsnorm(cq, q_norm_g) @ w_uq).reshape(B, T, B_HEADS, B_NOPE + B_ROPE)
    kv = (rmsnorm(ckv, kv_norm_g) @ w_ukv).reshape(B, T, B_HEADS, B_NOPE + B_V)
    q_nope, q_rope = q[..., :B_NOPE], q[..., B_NOPE:]
    k_nope, v = kv[..., :B_NOPE], kv[..., B_NOPE:]
    k_rope = kr[:, :, None, :]
    if rope is not None:
        q_rope = apply_rope(q_rope, *rope)
        k_rope = apply_rope(k_rope, *rope)
    q = jnp.concatenate([q_nope, q_rope], axis=-1)
    k = jnp.concatenate([k_nope, jnp.broadcast_to(k_rope, k_nope.shape[:-1] + (B_ROPE,))], axis=-1)
    return q, k, v


def mixer_ab(h, hc, w_in, gate_b, m_norm_g, q_norm_g, kv_norm_g, w_uq, w_ukv, rope, ctx_out):
    B, T, _ = h.shape
    mq, mk, mv, mo, mg, cq, ckv, kr = split_cols(h @ w_in, AB_SPLITS)
    mq_c, mk_c, mv_c, mo_c, mg_c, cq_c, ckv_c, kr_c = split_cols(hc @ w_in, AB_SPLITS)
    zero = mlstm_zero_state(B)
    q_scale = M_DK ** -0.5
    h_mc, ctx_states = mlstm_bidir(mlstm_heads(mq_c, M_DK) * q_scale, mlstm_heads(mk_c, M_DK),
                                   mlstm_heads(mv_c, M_DV), mlstm_gates(mg_c, gate_b), (zero, zero))
    h_ml, _ = mlstm_bidir(mlstm_heads(mq, M_DK) * q_scale, mlstm_heads(mk, M_DK),
                          mlstm_heads(mv, M_DV), mlstm_gates(mg, gate_b), ctx_states)
    m_lat = mlstm_out(h_ml, mo, m_norm_g)
    a_scale = (B_NOPE + B_ROPE) ** -0.5
    q_l, k_l, v_l = mla_qkv(cq, ckv, kr, q_norm_g, kv_norm_g, w_uq, w_ukv, rope)
    q_c, k_c, v_c = mla_qkv(cq_c, ckv_c, kr_c, q_norm_g, kv_norm_g, w_uq, w_ukv, None)
    b_lat = blocked_attention(q_l, jnp.concatenate([k_l, k_c], axis=1),
                              jnp.concatenate([v_l, v_c], axis=1), a_scale).reshape(B, T, B_WIDTH)
    y_lat = jnp.concatenate([m_lat, b_lat.astype(m_lat.dtype)], axis=-1)
    if not ctx_out:
        return y_lat, None
    m_ctx = mlstm_out(h_mc, mo_c, m_norm_g)
    b_ctx = dense_attention(q_c, k_c, v_c, a_scale).reshape(B, hc.shape[1], B_WIDTH)
    return y_lat, jnp.concatenate([m_ctx, b_ctx.astype(m_ctx.dtype)], axis=-1)


def na_mixer(h, hc, w_in, rel_bias, ctx_out):
    B, T, _ = h.shape
    rows = T // GRID_W
    H, d = C_HEADS, C_HEAD_DIM
    scale = d ** -0.5
    q, k, v = [a.reshape(B, T, H, d) for a in jnp.split(h @ w_in, 3, axis=-1)]
    qc, kc, vc = [a.reshape(B, hc.shape[1], H, d) for a in jnp.split(hc @ w_in, 3, axis=-1)]
    kh = min(NA_KH_MAX, rows)
    nwin = kh * NA_KW
    row_start = jnp.clip(jnp.arange(rows) - kh // 2, 0, rows - kh)
    col_idx = jnp.clip(jnp.arange(GRID_W) - NA_KW // 2, 0, GRID_W - NA_KW)[:, None] + jnp.arange(NA_KW)
    ci = col_idx - jnp.arange(GRID_W)[:, None] + NA_KW - 1
    k_grid = k.reshape(B, rows, GRID_W, H, d)
    v_grid = v.reshape(B, rows, GRID_W, H, d)
    q_rows = jnp.moveaxis(q.reshape(B, rows, GRID_W, H, d), 1, 0)

    def row_block(args):
        qr, r, rs = args
        kr = lax.dynamic_slice_in_dim(k_grid, rs, kh, axis=1)[:, :, col_idx]
        vr = lax.dynamic_slice_in_dim(v_grid, rs, kh, axis=1)[:, :, col_idx]
        ri = rs + jnp.arange(kh) - r + NA_KH_MAX - 1
        bias = rel_bias[:, ri[:, None, None], ci[None, :, :]]
        bias = bias.transpose(0, 2, 1, 3).reshape(H, GRID_W, nwin)
        s_win = jnp.einsum('bqhd,bxqyhd->bhqxy', qr, kr).reshape(B, H, GRID_W, nwin) * scale + bias
        s_ctx = jnp.einsum('bqhd,bkhd->bhqk', qr, kc) * scale
        p = jax.nn.softmax(jnp.concatenate([s_win, s_ctx], axis=-1).astype(jnp.float32), axis=-1)
        p = p.astype(v.dtype)
        p_win = p[..., :nwin].reshape(B, H, GRID_W, kh, NA_KW)
        return (jnp.einsum('bhqxy,bxqyhd->bqhd', p_win, vr)
                + jnp.einsum('bhqk,bkhd->bqhd', p[..., nwin:], vc))

    o = lax.map(row_block, (q_rows, jnp.arange(rows), row_start))
    y_lat = jnp.moveaxis(o, 0, 1).reshape(B, T, C_WIDTH)
    if not ctx_out:
        return y_lat, None
    y_ctx = dense_attention(qc, kc, vc, scale).reshape(B, hc.shape[1], C_WIDTH)
    return y_lat, y_ctx


def setup_inputs(seed: int = 0) -> dict:
    key = jax.random.key(seed)
    ks = jax.random.split(key, 24)
    f32 = jnp.float32
    nrm = lambda k, shape, s: jax.random.normal(k, shape, f32) * s
    gain = lambda k, shape: 1.0 + nrm(k, shape, 0.02)
    f_bias = jnp.linspace(3.0, 6.0, M_HEADS, dtype=f32)
    gate_base = jnp.concatenate([jnp.zeros((M_HEADS,), f32), f_bias, jnp.zeros((M_HEADS,), f32), f_bias])
    return {
        "x": nrm(ks[0], (BATCH, SEQ, D_MODEL), 1.0),
        "c": nrm(ks[1], (BATCH, D_MODEL), 1.0),
        "ctx": nrm(ks[2], (BATCH, CTX_LEN, D_MODEL), 1.0),
        "c_ctx": nrm(ks[3], (D_MODEL,), 1.0),
        "ada_w": nrm(ks[4], (DEPTH, D_MODEL, 6 * D_MODEL), 0.5 * D_MODEL ** -0.5),
        "ada_b": nrm(ks[5], (DEPTH, 6 * D_MODEL), 0.02),
        "norm1_g": gain(ks[6], (DEPTH, D_MODEL)),
        "norm2_g": gain(ks[7], (DEPTH, D_MODEL)),
        "mlp_w1": nrm(ks[8], (DEPTH, D_MODEL, D_FF), D_MODEL ** -0.5),
        "mlp_w2": nrm(ks[9], (DEPTH, D_FF, D_MODEL), D_FF ** -0.5),
        "ab_w_in": nrm(ks[10], (N_EVEN, D_MODEL, AB_IN), D_MODEL ** -0.5),
        "ab_gate_b": gate_base + nrm(ks[11], (N_EVEN, 4 * M_HEADS), 0.1),
        "ab_m_norm_g": gain(ks[12], (N_EVEN, M_HEADS, M_DV)),
        "ab_q_norm_g": gain(ks[13], (N_EVEN, B_Q_RANK)),
        "ab_kv_norm_g": gain(ks[14], (N_EVEN, B_KV_RANK)),
        "ab_w_uq": nrm(ks[15], (N_EVEN, B_Q_RANK, B_HEADS * (B_NOPE + B_ROPE)), B_Q_RANK ** -0.5),
        "ab_w_ukv": nrm(ks[16], (N_EVEN, B_KV_RANK, B_HEADS * (B_NOPE + B_V)), B_KV_RANK ** -0.5),
        "ab_w_out": nrm(ks[17], (N_EVEN, AB_WIDTH, D_MODEL), AB_WIDTH ** -0.5),
        "na_w_in": nrm(ks[18], (N_ODD, D_MODEL, 3 * C_WIDTH), D_MODEL ** -0.5),
        "na_rel_bias": nrm(ks[19], (N_ODD, C_HEADS, 2 * NA_KH_MAX - 1, 2 * NA_KW - 1), 0.5),
        "na_w_out": nrm(ks[20], (N_ODD, C_WIDTH, D_MODEL), C_WIDTH ** -0.5),
        "final_norm_g": gain(ks[21], (D_MODEL,)),
    }


def reference(x, c, ctx, c_ctx, ada_w, ada_b, norm1_g, norm2_g, mlp_w1, mlp_w2,
              ab_w_in, ab_gate_b, ab_m_norm_g, ab_q_norm_g, ab_kv_norm_g, ab_w_uq, ab_w_ukv, ab_w_out,
              na_w_in, na_rel_bias, na_w_out, final_norm_g):
    T = x.shape[1]
    rope = rope_2d(T)
    for i in range(DEPTH):
        ctx_out = i < DEPTH - 1
        mod = jax.nn.silu(c) @ ada_w[i] + ada_b[i]
        mod_c = jax.nn.silu(c_ctx) @ ada_w[i] + ada_b[i]
        sh1, sc1, g1, sh2, sc2, g2 = jnp.split(mod[:, None, :], 6, axis=-1)
        csh1, csc1, cg1, csh2, csc2, cg2 = jnp.split(mod_c, 6, axis=-1)
        h = modulate(x, norm1_g[i], sh1, sc1)
        hc = modulate(ctx, norm1_g[i], csh1, csc1)
        if i % 2 == 0:
            e = i // 2
            y, yc = mixer_ab(h, hc, ab_w_in[e], ab_gate_b[e], ab_m_norm_g[e], ab_q_norm_g[e],
                             ab_kv_norm_g[e], ab_w_uq[e], ab_w_ukv[e], rope, ctx_out)
            w_out = ab_w_out[e]
        else:
            o = i // 2
            y, yc = na_mixer(h, hc, na_w_in[o], na_rel_bias[o], ctx_out)
            w_out = na_w_out[o]
        x = x + g1 * (y @ w_out)
        x = x + g2 * sqrelu_mlp(modulate(x, norm2_g[i], sh2, sc2), mlp_w1[i], mlp_w2[i])
        if ctx_out:
            ctx = ctx + cg1 * (yc @ w_out)
            ctx = ctx + cg2 * sqrelu_mlp(modulate(ctx, norm2_g[i], csh2, csc2), mlp_w1[i], mlp_w2[i])
    return rmsnorm(x, final_norm_g)
```

```python
import functools

import numpy as np
import jax
import jax.numpy as jnp
from jax import lax
from jax.experimental import pallas as pl
from jax.experimental.pallas import tpu as pltpu

F32 = jnp.float32
BF16 = jnp.bfloat16

D_MODEL = 1024
DEPTH = 2
GRID_W = 64
EPS = 1e-6
M_HEADS = 4
M_DK = 128
M_DV = 128
B_HEADS = 8
B_Q_RANK = 256
B_KV_RANK = 128
B_NOPE = 64
B_ROPE = 32
B_V = 64
ROPE_BASE = 10000.0
C_HEADS = 16
C_HEAD_DIM = 64
NA_KH = 8
NA_KW = 16
D_FF = 4 * D_MODEL
M_WIDTH = M_HEADS * M_DV
B_WIDTH = B_HEADS * B_V

LANES = 128
NEG = -1e30
VMEM_LIMIT = 56 * 1024 * 1024

P_Q, P_K, P_V, P_O = 0, 512, 1024, 1536
P_CQ, P_CKV, P_KR1, P_KR2 = 2048, 2304, 2432, 2560
P_MAIN = 2688
P_GATES = 256
MLSTM_CHUNK = 256
NV = 2 * M_DV


def _cparams(sem):
    return pltpu.CompilerParams(dimension_semantics=sem, vmem_limit_bytes=VMEM_LIMIT)


def _rms(x):
    return x * lax.rsqrt(jnp.mean(x * x, axis=-1, keepdims=True) + EPS)


def _dot(a, b):
    return jnp.dot(a, b, preferred_element_type=F32)


def _dot_nt(a, b):
    return lax.dot_general(a, b, (((1,), (1,)), ((), ())), preferred_element_type=F32)


def _ada_kernel(c_ref, w_ref, b_ref, o_ref):
    c = c_ref[...]
    s = c * jax.nn.sigmoid(c)
    o_ref[...] = jnp.dot(s, w_ref[...], precision=lax.Precision.HIGHEST,
                         preferred_element_type=F32) + b_ref[...]


def _ada(c_rows, ada_w, ada_b):
    depth, d, n = ada_w.shape
    tn = 1536
    return pl.pallas_call(
        _ada_kernel,
        grid=(depth, n // tn),
        in_specs=[pl.BlockSpec((8, d), lambda l, j: (0, 0)),
                  pl.BlockSpec((None, d, tn), lambda l, j: (l, 0, j)),
                  pl.BlockSpec((None, 1, tn), lambda l, j: (l, 0, j))],
        out_specs=pl.BlockSpec((None, 8, tn), lambda l, j: (l, 0, j)),
        out_shape=jax.ShapeDtypeStruct((depth, 8, n), F32),
        compiler_params=_cparams(("arbitrary", "arbitrary")),
        name="ada",
    )(c_rows, ada_w, ada_b.reshape(depth, 1, n))


def _modproj_kernel(x_ref, sh_ref, sc_ref, g_ref, w_ref, *o_refs, widths, chunk):
    x = x_ref[...]
    h = _rms(x) * g_ref[...] * (1.0 + sc_ref[...]) + sh_ref[...]
    hb = h.astype(BF16)
    off = 0
    for o_ref, n in zip(o_refs, widths):
        for c0 in range(0, n, chunk):
            c1 = min(n, c0 + chunk)
            o_ref[:, c0:c1] = _dot(hb, w_ref[:, off + c0:off + c1]).astype(o_ref.dtype)
        off += n


def _modproj(x, shift, scale, g, w, widths, dtypes, tm):
    b, t, d = x.shape
    bm = shift.shape[0]
    mod_map = (lambda i, j: (i, 0, 0)) if bm == b else (lambda i, j: (0, 0, 0))
    tm = min(tm, t)
    return pl.pallas_call(
        functools.partial(_modproj_kernel, widths=widths, chunk=512),
        grid=(b, t // tm),
        in_specs=[pl.BlockSpec((None, tm, d), lambda i, j: (i, j, 0)),
                  pl.BlockSpec((None, 1, d), mod_map),
                  pl.BlockSpec((None, 1, d), mod_map),
                  pl.BlockSpec((1, d), lambda i, j: (0, 0)),
                  pl.BlockSpec(w.shape, lambda i, j: (0, 0))],
        out_specs=[pl.BlockSpec((None, tm, n), lambda i, j: (i, j, 0)) for n in widths],
        out_shape=[jax.ShapeDtypeStruct((b, t, n), dt) for n, dt in zip(widths, dtypes)],
        compiler_params=_cparams(("parallel", "parallel")),
        name="modproj",
    )(x, shift, scale, g.reshape(1, d), w)


def _mlstm_kernel(*refs, has_init, emit_state):
    q_ref, k_ref, v_ref, g_ref, gb_ref = refs[:5]
    pos = 5
    if has_init:
        c0_ref, m0_ref = refs[pos:pos + 2]
        pos += 2
    h_ref = refs[pos]
    pos += 1
    if emit_state:
        cout_ref, mout_ref = refs[pos:pos + 2]
        pos += 2
    c_s, m_s = refs[pos:pos + 2]

    d = pl.program_id(1)
    j = pl.program_id(2)
    L = q_ref.shape[0]

    @pl.when(j == 0)
    def _():
        if has_init:
            c_s[...] = c0_ref[...]
            m_s[...] = m0_ref[...]
        else:
            c_s[...] = jnp.zeros_like(c_s)
            m_s[...] = jnp.zeros_like(m_s)

    r = lax.broadcasted_iota(jnp.int32, (L, L), 0)
    c = lax.broadcasted_iota(jnp.int32, (L, L), 1)
    sgn = 1 - 2 * d
    keep = (r - c) * sgn >= 0
    tri = keep.astype(F32)

    gates = g_ref[...] + gb_ref[...]
    ls = jax.nn.log_sigmoid(gates)
    gates_t = gates.T[0:8]
    ls_t = jax.nn.log_sigmoid(gates_t)
    hi = lax.Precision.HIGHEST
    b_cols = jnp.dot(tri, ls, precision=hi, preferred_element_type=F32)
    b_rows = lax.dot_general(ls_t, tri, (((1,), (1,)), ((), ())), precision=hi,
                             preferred_element_type=F32)
    b_tot = jnp.sum(ls, axis=0, keepdims=True)

    lane = lax.broadcasted_iota(jnp.int32, (L, M_DV), 1)
    ones_col = (lane == 0).astype(BF16)
    scale = M_DK ** -0.5

    for h in range(M_HEADS):
        sl = slice(h * M_DK, (h + 1) * M_DK)
        q = q_ref[:, sl]
        k = k_ref[:, sl]
        v = v_ref[:, sl]
        vext = jnp.concatenate([v, ones_col], axis=-1)
        cn = c_s[h]
        m = m_s[h][0:1, 0:1]

        li_c = gates[:, h:h + 1]
        b_c = b_cols[:, 4 + h:5 + h]
        li_r = gates_t[h:h + 1, :]
        b_r = b_rows[4 + h:5 + h, :]
        b_last = b_tot[:, 4 + h:5 + h]

        dm = jnp.where(keep, b_c - b_r + li_r, NEG)
        inter = b_c + m
        m_t = jnp.maximum(jnp.max(dm, axis=-1, keepdims=True), inter)
        s = _dot_nt(q, k) * (scale * jnp.exp(dm - m_t))
        w_inter = jnp.exp(inter - m_t) * scale
        nd = w_inter * _dot(q, cn.astype(BF16)) + _dot(s.astype(BF16), vext)
        den = nd[:, M_DV:M_DV + 1]
        hh = nd[:, :M_DV] / jnp.maximum(jnp.abs(den), jnp.exp(-m_t))
        h_ref[:, sl] = hh

        g_c = b_last - b_c + li_c
        m_new = jnp.maximum(b_last + m, jnp.max(g_c, axis=0, keepdims=True))
        w_k = jnp.exp(g_c - m_new)
        decay = jnp.exp(b_last + m - m_new)
        kw = (k.astype(F32) * w_k).astype(BF16)
        upd = lax.dot_general(kw, vext, (((0,), (0,)), ((), ())), preferred_element_type=F32)
        c_s[h] = decay * cn + upd
        m_s[h] = jnp.broadcast_to(m_new, m_s.shape[1:])

    if emit_state:
        cout_ref[...] = c_s[...]
        mout_ref[...] = m_s[...]


def _mlstm(p_main, p_gates, gate_b2, init, emit_state):
    b, t, _ = p_main.shape
    L = min(MLSTM_CHUNK, t)
    nc = t // L

    def cidx(d, j):
        return j + d * (nc - 1 - 2 * j)

    def col(cb):
        return pl.BlockSpec((None, L, M_WIDTH), lambda i, d, j: (i, cidx(d, j), cb))

    in_specs = [col(P_Q // M_WIDTH), col(P_K // M_WIDTH), col(P_V // M_WIDTH),
                pl.BlockSpec((None, L, LANES), lambda i, d, j: (i, cidx(d, j), d)),
                pl.BlockSpec((None, 1, LANES), lambda i, d, j: (d, 0, 0))]
    args = [p_main, p_main, p_main, p_gates, gate_b2]
    st_c = pl.BlockSpec((None, None, M_HEADS, M_DK, NV), lambda i, d, j: (i, d, 0, 0, 0))
    st_m = pl.BlockSpec((None, None, M_HEADS, 8, LANES), lambda i, d, j: (i, d, 0, 0, 0))
    if init is not None:
        in_specs += [st_c, st_m]
        args += list(init)
    out_specs = [pl.BlockSpec((None, None, L, M_WIDTH), lambda i, d, j: (d, i, cidx(d, j), 0))]
    out_shape = [jax.ShapeDtypeStruct((2, b, t, M_WIDTH), F32)]
    if emit_state:
        out_specs += [st_c, st_m]
        out_shape += [jax.ShapeDtypeStruct((b, 2, M_HEADS, M_DK, NV), F32),
                      jax.ShapeDtypeStruct((b, 2, M_HEADS, 8, LANES), F32)]
    return pl.pallas_call(
        functools.partial(_mlstm_kernel, has_init=init is not None, emit_state=emit_state),
        grid=(b, 2, nc),
        in_specs=in_specs,
        out_specs=out_specs,
        out_shape=out_shape,
        scratch_shapes=[pltpu.VMEM((M_HEADS, M_DK, NV), F32),
                        pltpu.VMEM((M_HEADS, 8, LANES), F32)],
        compiler_params=_cparams(("parallel", "parallel", "arbitrary")),
        name="mlstm",
    )(*args)


def _mlaprep_kernel(cq_ref, ckv_ref, kr1_ref, kr2_ref, cos_ref, sin_ref, qg_ref, kvg_ref,
                    wq1_ref, wq2_ref, wk_ref, wv_ref, q_ref, k_ref, v_ref):
    cos = cos_ref[...]
    sin = sin_ref[...]
    cq = cq_ref[...].astype(F32)
    cqn = (_rms(cq) * qg_ref[...]).astype(BF16)
    ckv = ckv_ref[...].astype(F32)
    ckvn = (_rms(ckv) * kvg_ref[...]).astype(BF16)
    a_scale = (B_NOPE + B_ROPE) ** -0.5
    kr = kr1_ref[...].astype(F32) * cos + kr2_ref[...].astype(F32) * sin
    v_ref[...] = _dot(ckvn, wv_ref[...]).astype(v_ref.dtype)
    for h in range(B_HEADS):
        sl = slice(h * LANES, (h + 1) * LANES)
        q1 = _dot(cqn, wq1_ref[:, sl])
        q2 = _dot(cqn, wq2_ref[:, sl])
        q_ref[:, sl] = ((q1 * cos + q2 * sin) * a_scale).astype(q_ref.dtype)
        k_ref[:, sl] = (_dot(ckvn, wk_ref[:, sl]) + kr).astype(k_ref.dtype)


def _mlaprep(p_main, cos, sin, qg, kvg, wq1, wq2, wk, wv, tm):
    b, t, _ = p_main.shape
    tm = min(tm, t)
    hw = B_HEADS * LANES

    def tok(width, cb):
        return pl.BlockSpec((None, tm, width), lambda i, j: (i, j, cb))

    def full(a):
        return pl.BlockSpec(a.shape, lambda i, j: (0,) * a.ndim)

    return pl.pallas_call(
        _mlaprep_kernel,
        grid=(b, t // tm),
        in_specs=[tok(B_Q_RANK, P_CQ // B_Q_RANK), tok(LANES, P_CKV // LANES),
                  tok(LANES, P_KR1 // LANES), tok(LANES, P_KR2 // LANES),
                  pl.BlockSpec((tm, LANES), lambda i, j: (j, 0)),
                  pl.BlockSpec((tm, LANES), lambda i, j: (j, 0)),
                  full(qg), full(kvg), full(wq1), full(wq2), full(wk), full(wv)],
        out_specs=[pl.BlockSpec((None, tm, hw), lambda i, j: (i, j, 0)),
                   pl.BlockSpec((None, tm, hw), lambda i, j: (i, j, 0)),
                   pl.BlockSpec((None, tm, B_WIDTH), lambda i, j: (i, j, 0))],
        out_shape=[jax.ShapeDtypeStruct((b, t, hw), BF16),
                   jax.ShapeDtypeStruct((b, t, hw), BF16),
                   jax.ShapeDtypeStruct((b, t, B_WIDTH), BF16)],
        compiler_params=_cparams(("parallel", "parallel")),
        name="mlaprep",
    )(p_main, p_main, p_main, p_main, cos, sin, qg, kvg, wq1, wq2, wk, wv)


def _flash_kernel(*refs, tk, nk, has_ctx):
    if has_ctx:
        q_ref, k_ref, v_ref, kc_ref, vc_ref, o_ref = refs
    else:
        q_ref, k_ref, v_ref, o_ref = refs
    tq = q_ref.shape[0]
    outs = []
    for h in range(2):
        sl = slice(h * LANES, (h + 1) * LANES)
        q = q_ref[:, sl]

        def step(kb, vb, carry):
            m, l, acc = carry
            s = _dot_nt(q, kb)
            m_new = jnp.maximum(m, jnp.max(s, axis=-1, keepdims=True))
            alpha = jnp.exp(m - m_new)
            p = jnp.exp(s - m_new)
            l = alpha * l + jnp.sum(p, axis=-1, keepdims=True)
            acc = alpha * acc + _dot(p.astype(BF16), vb)
            return m_new, l, acc

        def body(j, carry, sl=sl, step=step):
            off = pl.multiple_of(j * tk, tk)
            return step(k_ref[pl.ds(off, tk), sl], v_ref[pl.ds(off, tk), :], carry)

        carry = (jnp.full((tq, 1), NEG, F32), jnp.zeros((tq, 1), F32),
                 jnp.zeros((tq, LANES), F32))
        carry = lax.fori_loop(0, nk, body, carry)
        if has_ctx:
            carry = step(kc_ref[:, sl], vc_ref[...], carry)
        _, l, acc = carry
        outs.append(acc / l)
    lane = lax.broadcasted_iota(jnp.int32, (tq, LANES), 1)
    o_ref[...] = jnp.where(lane < B_V, outs[0], outs[1]).astype(o_ref.dtype)


def _flash(q, k, v, kc, vc, tq, tk):
    b, t, _ = q.shape
    tkeys = k.shape[1]
    tq = min(tq, t)
    tk = min(tk, tkeys)
    has_ctx = kc is not None
    in_specs = [pl.BlockSpec((None, tq, 2 * LANES), lambda i, hp, j: (i, j, hp)),
                pl.BlockSpec((None, tkeys, 2 * LANES), lambda i, hp, j: (i, 0, hp)),
                pl.BlockSpec((None, tkeys, LANES), lambda i, hp, j: (i, 0, hp))]
    args = [q, k, v]
    if has_ctx:
        tc = kc.shape[1]
        in_specs += [pl.BlockSpec((None, tc, 2 * LANES), lambda i, hp, j: (i, 0, hp)),
                     pl.BlockSpec((None, tc, LANES), lambda i, hp, j: (i, 0, hp))]
        args += [kc, vc]
    return pl.pallas_call(
        functools.partial(_flash_kernel, tk=tk, nk=tkeys // tk, has_ctx=has_ctx),
        grid=(b, B_HEADS // 2, t // tq),
        in_specs=in_specs,
        out_specs=pl.BlockSpec((None, tq, LANES), lambda i, hp, j: (i, j, hp)),
        out_shape=jax.ShapeDtypeStruct((b, t, B_WIDTH), BF16),
        compiler_params=_cparams(("parallel", "parallel", "arbitrary")),
        name="flash",
    )(*args)


def _natten_kernel(q_ref, k_ref, v_ref, kc_ref, vc_ref, bias_ref, o_ref, *, rb, rows):
    blk = pl.program_id(2)
    kc = kc_ref[...]
    vc = vc_ref[...]
    lane = lax.broadcasted_iota(jnp.int32, (GRID_W, LANES), 1)
    scale = C_HEAD_DIM ** -0.5
    win = NA_KH * GRID_W

    def row(i, carry):
        r = blk * rb + i
        rs = jnp.clip(r - NA_KH // 2, 0, rows - NA_KH)
        var = r - rs
        qr = q_ref[pl.ds(pl.multiple_of(i * GRID_W, GRID_W), GRID_W), :]
        koff = pl.multiple_of(rs * GRID_W, GRID_W)
        kw = k_ref[pl.ds(koff, win), :]
        vw = v_ref[pl.ds(koff, win), :]
        outs = []
        for h in range(2):
            in_head = (lane >= h * C_HEAD_DIM) & (lane < (h + 1) * C_HEAD_DIM)
            qh = jnp.where(in_head, qr, jnp.zeros_like(qr))
            s_w = _dot_nt(qh, kw) * scale + bias_ref[h, var]
            s_c = _dot_nt(qh, kc) * scale
            m = jnp.maximum(jnp.max(s_w, axis=-1, keepdims=True),
                            jnp.max(s_c, axis=-1, keepdims=True))
            p_w = jnp.exp(s_w - m)
            p_c = jnp.exp(s_c - m)
            l = jnp.sum(p_w, axis=-1, keepdims=True) + jnp.sum(p_c, axis=-1, keepdims=True)
            o = _dot(p_w.astype(BF16), vw) + _dot(p_c.astype(BF16), vc)
            outs.append(o / l)
        o_ref[pl.ds(pl.multiple_of(i * GRID_W, GRID_W), GRID_W), :] = jnp.where(
            lane < C_HEAD_DIM, outs[0], outs[1]).astype(o_ref.dtype)
        return carry

    lax.fori_loop(0, rb, row, 0)


def _natten(qkv, qkv_c, bias_t, rb):
    b, t, _ = qkv.shape
    tc = qkv_c.shape[1]
    rows = t // GRID_W
    rb = min(rb, rows)
    nhp = C_HEADS // 2
    return pl.pallas_call(
        functools.partial(_natten_kernel, rb=rb, rows=rows),
        grid=(b, nhp, rows // rb),
        in_specs=[pl.BlockSpec((None, rb * GRID_W, LANES), lambda i, hp, j: (i, j, hp)),
                  pl.BlockSpec((None, t, LANES), lambda i, hp, j: (i, 0, nhp + hp)),
                  pl.BlockSpec((None, t, LANES), lambda i, hp, j: (i, 0, 2 * nhp + hp)),
                  pl.BlockSpec((None, tc, LANES), lambda i, hp, j: (i, 0, nhp + hp)),
                  pl.BlockSpec((None, tc, LANES), lambda i, hp, j: (i, 0, 2 * nhp + hp)),
                  pl.BlockSpec((None,) + bias_t.shape[1:], lambda i, hp, j: (hp, 0, 0, 0, 0))],
        out_specs=pl.BlockSpec((None, rb * GRID_W, LANES), lambda i, hp, j: (i, j, hp)),
        out_shape=jax.ShapeDtypeStruct((b, t, C_HEADS * C_HEAD_DIM), BF16),
        compiler_params=_cparams(("parallel", "parallel", "arbitrary")),
        name="natten",
    )(qkv, qkv, qkv, qkv_c, qkv_c, bias_t)


def _natten_bias_table(rel_bias):
    var = np.arange(NA_KH)[:, None]
    x = np.arange(NA_KH)[None, :]
    ri = x - var + NA_KH - 1
    qc = np.arange(GRID_W)[:, None]
    kcol = np.arange(GRID_W)[None, :]
    cs = np.clip(qc - NA_KW // 2, 0, GRID_W - NA_KW)
    valid = (kcol >= cs) & (kcol < cs + NA_KW)
    ci = np.clip(kcol - qc + NA_KW - 1, 0, 2 * NA_KW - 2)
    tab = rel_bias[:, ri[:, :, None, None], ci[None, None, :, :]]
    tab = jnp.where(valid[None, None, None], tab, NEG)
    tab = tab.transpose(0, 1, 3, 2, 4).reshape(C_HEADS // 2, 2, NA_KH, GRID_W, NA_KH * GRID_W)
    return tab.astype(F32)


def _mlp_tail(x1, g2, sh2, sc2, n2g, w1_ref, w2_ref):
    h = (_rms(x1) * n2g * (1.0 + sc2) + sh2).astype(BF16)
    acc = jnp.zeros_like(x1)
    chunk = 1024
    for c0 in range(0, D_FF, chunk):
        a = jnp.maximum(_dot(h, w1_ref[:, c0:c0 + chunk]), 0.0)
        acc = acc + _dot((a * a).astype(BF16), w2_ref[c0:c0 + chunk, :])
    return x1 + g2 * acc


def _post_ab_kernel(x_ref, hf_ref, hb_ref, og_ref, bl_ref, mg_ref, g1_ref, sh2_ref, sc2_ref,
                    g2_ref, n2g_ref, wo_ref, w1_ref, w2_ref, o_ref):
    hm = hf_ref[...] + hb_ref[...]
    og = jax.nn.sigmoid(og_ref[...].astype(F32))
    mg = mg_ref[...]
    y = jnp.zeros(x_ref.shape, F32)
    for h in range(M_HEADS):
        sl = slice(h * M_DV, (h + 1) * M_DV)
        hn = (_rms(hm[:, sl]) * mg[:, sl] * og[:, sl]).astype(BF16)
        y = y + _dot(hn, wo_ref[sl, :])
    y = y + _dot(bl_ref[...], wo_ref[M_WIDTH:, :])
    x1 = x_ref[...] + g1_ref[...] * y
    o_ref[...] = _mlp_tail(x1, g2_ref[...], sh2_ref[...], sc2_ref[...], n2g_ref[...],
                           w1_ref, w2_ref)


def _post_c_kernel(x_ref, y_ref, g1_ref, sh2_ref, sc2_ref, g2_ref, n2g_ref, fg_ref,
                   wo_ref, w1_ref, w2_ref, o_ref):
    x1 = x_ref[...] + g1_ref[...] * _dot(y_ref[...], wo_ref[...])
    x2 = _mlp_tail(x1, g2_ref[...], sh2_ref[...], sc2_ref[...], n2g_ref[...], w1_ref, w2_ref)
    o_ref[...] = _rms(x2) * fg_ref[...]


def _resident(a):
    return pl.BlockSpec(a.shape, lambda i, j: (0,) * a.ndim, pipeline_mode=pl.Buffered(1))


def _post_ab(x, h_dirs, p_main, b_lat, mg, mods, n2g, wo, w1, w2, tm):
    b, t, d = x.shape
    tm = min(tm, t)
    g1, sh2, sc2, g2 = mods
    bm = g1.shape[0]
    mod_map = (lambda i, j: (i, 0, 0)) if bm == b else (lambda i, j: (0, 0, 0))
    mod_spec = pl.BlockSpec((None, 1, d), mod_map)
    tok = lambda w, cb: pl.BlockSpec((None, tm, w), lambda i, j: (i, j, cb))
    return pl.pallas_call(
        _post_ab_kernel,
        grid=(b, t // tm),
        in_specs=[tok(d, 0),
                  pl.BlockSpec((None, None, tm, M_WIDTH), lambda i, j: (0, i, j, 0)),
                  pl.BlockSpec((None, None, tm, M_WIDTH), lambda i, j: (1, i, j, 0)),
                  tok(M_WIDTH, P_O // M_WIDTH), tok(B_WIDTH, 0),
                  pl.BlockSpec((1, M_WIDTH), lambda i, j: (0, 0)),
                  mod_spec, mod_spec, mod_spec, mod_spec,
                  pl.BlockSpec((1, d), lambda i, j: (0, 0)),
                  _resident(wo), _resident(w1), _resident(w2)],
        out_specs=tok(d, 0),
        out_shape=jax.ShapeDtypeStruct((b, t, d), F32),
        compiler_params=_cparams(("parallel", "parallel")),
        name="post_ab",
    )(x, h_dirs, h_dirs, p_main, b_lat, mg, g1, sh2, sc2, g2, n2g.reshape(1, d), wo, w1, w2)


def _post_c(x, y, mods, n2g, fg, wo, w1, w2, tm):
    b, t, d = x.shape
    tm = min(tm, t)
    g1, sh2, sc2, g2 = mods
    mod_spec = pl.BlockSpec((None, 1, d), lambda i, j: (i, 0, 0))
    tok = pl.BlockSpec((None, tm, d), lambda i, j: (i, j, 0))
    vec = pl.BlockSpec((1, d), lambda i, j: (0, 0))
    return pl.pallas_call(
        _post_c_kernel,
        grid=(b, t // tm),
        in_specs=[tok, tok, mod_spec, mod_spec, mod_spec, mod_spec, vec, vec,
                  _resident(wo), _resident(w1), _resident(w2)],
        out_specs=tok,
        out_shape=jax.ShapeDtypeStruct((b, t, d), F32),
        compiler_params=_cparams(("parallel", "parallel")),
        name="post_c",
    )(x, y, g1, sh2, sc2, g2, n2g.reshape(1, d), fg.reshape(1, d), wo, w1, w2)


def _rope_tables(t):
    pos = np.arange(t)
    row = (pos // GRID_W).astype(np.float32)
    colp = (pos % GRID_W).astype(np.float32)
    n_f = B_ROPE // 4
    freqs = (ROPE_BASE ** (-np.arange(n_f, dtype=np.float32) / n_f)).astype(np.float32)
    ang = np.concatenate([row[:, None] * freqs, colp[:, None] * freqs], axis=-1)
    cos = np.zeros((t, LANES), np.float32)
    sin = np.zeros((t, LANES), np.float32)
    cos[:, :B_NOPE] = 1.0
    cos[:, B_NOPE:B_NOPE + 16] = np.cos(ang)
    cos[:, B_NOPE + 16:B_NOPE + 32] = np.cos(ang)
    sin[:, B_NOPE:B_NOPE + 16] = np.sin(ang)
    sin[:, B_NOPE + 16:B_NOPE + 32] = np.sin(ang)
    return jnp.asarray(cos), jnp.asarray(sin)


def _identity_tables(t):
    cos = np.zeros((t, LANES), np.float32)
    cos[:, :B_NOPE + B_ROPE] = 1.0
    return jnp.asarray(cos), jnp.zeros((t, LANES), F32)


def _layer0_weights(w_in, gate_b, w_uq, w_ukv):
    d = w_in.shape[0]
    half = B_ROPE // 2
    o_g = 4 * M_WIDTH
    o_cq = o_g + 4 * M_HEADS
    o_ckv = o_cq + B_Q_RANK
    o_kr = o_ckv + B_KV_RANK
    kr = w_in[:, o_kr:o_kr + B_ROPE]
    z = lambda n: jnp.zeros((d, n), w_in.dtype)
    kr1 = jnp.concatenate([z(B_NOPE), kr, z(LANES - B_NOPE - B_ROPE)], axis=1)
    kr2 = jnp.concatenate([z(B_NOPE), -kr[:, half:], kr[:, :half], z(LANES - B_NOPE - B_ROPE)], axis=1)
    wg = w_in[:, o_g:o_g + 4 * M_HEADS]
    gpad = z(LANES - 2 * M_HEADS)
    w0 = jnp.concatenate([w_in[:, :o_g], w_in[:, o_cq:o_kr], kr1, kr2,
                          wg[:, :2 * M_HEADS], gpad, wg[:, 2 * M_HEADS:], gpad], axis=1).astype(BF16)
    gb = jnp.zeros((2, 1, LANES), F32)
    gb = gb.at[0, 0, :2 * M_HEADS].set(gate_b[:2 * M_HEADS]).at[1, 0, :2 * M_HEADS].set(gate_b[2 * M_HEADS:])

    qr = B_Q_RANK
    wq = w_uq.reshape(qr, B_HEADS, B_NOPE + B_ROPE)
    nope, ra, rb = wq[..., :B_NOPE], wq[..., B_NOPE:B_NOPE + half], wq[..., B_NOPE + half:]
    zq = jnp.zeros((qr, B_HEADS, LANES - B_NOPE - B_ROPE), w_uq.dtype)
    wq1 = jnp.concatenate([nope, ra, rb, zq], axis=-1).reshape(qr, B_HEADS * LANES).astype(BF16)
    wq2 = jnp.concatenate([jnp.zeros_like(nope), -rb, ra, zq], axis=-1).reshape(qr, B_HEADS * LANES).astype(BF16)
    wkv = w_ukv.reshape(B_KV_RANK, B_HEADS, B_NOPE + B_V)
    wk = jnp.concatenate([wkv[..., :B_NOPE], jnp.zeros((B_KV_RANK, B_HEADS, LANES - B_NOPE), w_ukv.dtype)],
                         axis=-1).reshape(B_KV_RANK, B_HEADS * LANES).astype(BF16)
    wv = wkv[..., B_NOPE:].reshape(B_KV_RANK, B_WIDTH).astype(BF16)
    return w0, gb, wq1, wq2, wk, wv


def _split_mod(mod_rows):
    return [m[:, None, :] for m in jnp.split(mod_rows, 6, axis=-1)]


def kernel(x, c, ctx, c_ctx, ada_w, ada_b, norm1_g, norm2_g, mlp_w1, mlp_w2, ab_w_in, ab_gate_b,
           ab_m_norm_g, ab_q_norm_g, ab_kv_norm_g, ab_w_uq, ab_w_ukv, ab_w_out, na_w_in,
           na_rel_bias, na_w_out, final_norm_g):
    b, t, d = x.shape
    tc = ctx.shape[1]
    c_rows = jnp.concatenate([c, c_ctx[None, :], jnp.zeros((8 - b - 1, d), F32)], axis=0)
    mod = _ada(c_rows, ada_w, ada_b)

    sh1, sc1, g1, sh2, sc2, g2 = _split_mod(mod[0, :b])
    csh1, csc1, cg1, csh2, csc2, cg2 = _split_mod(mod[0, b:b + 1])
    w0, gb, wq1, wq2, wk, wv = _layer0_weights(ab_w_in[0], ab_gate_b[0], ab_w_uq[0], ab_w_ukv[0])
    widths, dtypes = (P_MAIN, P_GATES), (BF16, F32)
    p_main, p_gates = _modproj(x, sh1, sc1, norm1_g[0], w0, widths, dtypes, tm=512)
    pc_main, pc_gates = _modproj(ctx, csh1, csc1, norm1_g[0], w0, widths, dtypes, tm=256)

    hc_dirs, st_c, st_m = _mlstm(pc_main, pc_gates, gb, None, True)
    (h_dirs,) = _mlstm(p_main, p_gates, gb, (st_c, st_m), False)

    qg = ab_q_norm_g[0].reshape(1, -1)
    kvg = ab_kv_norm_g[0].reshape(1, -1)
    cos, sin = _rope_tables(t)
    cos_c, sin_c = _identity_tables(tc)
    q_l, k_l, v_l = _mlaprep(p_main, cos, sin, qg, kvg, wq1, wq2, wk, wv, tm=512)
    q_c, k_c, v_c = _mlaprep(pc_main, cos_c, sin_c, qg, kvg, wq1, wq2, wk, wv, tm=256)
    b_lat = _flash(q_l, k_l, v_l, k_c, v_c, tq=256, tk=512)
    b_ctx = _flash(q_c, k_c, v_c, None, None, tq=256, tk=256)

    wo = ab_w_out[0].astype(BF16)
    w1 = mlp_w1[0].astype(BF16)
    w2 = mlp_w2[0].astype(BF16)
    mg = ab_m_norm_g[0].reshape(1, M_WIDTH)
    x = _post_ab(x, h_dirs, p_main, b_lat, mg, (g1, sh2, sc2, g2), norm2_g[0], wo, w1, w2, tm=512)
    ctx = _post_ab(ctx, hc_dirs, pc_main, b_ctx, mg, (cg1, csh2, csc2, cg2), norm2_g[0],
                   wo, w1, w2, tm=256)

    sh1, sc1, g1, sh2, sc2, g2 = _split_mod(mod[1, :b])
    csh1, csc1 = _split_mod(mod[1, b:b + 1])[:2]
    wn = na_w_in[0].astype(BF16)
    nw = (wn.shape[1],)
    (qkv,) = _modproj(x, sh1, sc1, norm1_g[1], wn, nw, (BF16,), tm=512)
    (qkv_c,) = _modproj(ctx, csh1, csc1, norm1_g[1], wn, nw, (BF16,), tm=256)
    bias_t = _natten_bias_table(na_rel_bias[0])
    y = _natten(qkv, qkv_c, bias_t, rb=8)
    return _post_c(x, y, (g1, sh2, sc2, g2), norm2_g[1], final_norm_g,
                   na_w_out[0].astype(BF16), mlp_w1[1].astype(BF16), mlp_w2[1].astype(BF16), tm=512)
```

```python
import functools

import numpy as np
import jax
import jax.numpy as jnp
from jax import lax
from jax.experimental import pallas as pl
from jax.experimental.pallas import tpu as pltpu

F32 = jnp.float32
BF16 = jnp.bfloat16

D_MODEL = 1024
DEPTH = 2
GRID_W = 64
EPS = 1e-6
M_HEADS = 4
M_DK = 128
M_DV = 128
B_HEADS = 8
B_Q_RANK = 256
B_KV_RANK = 128
B_NOPE = 64
B_ROPE = 32
B_V = 64
ROPE_BASE = 10000.0
C_HEADS = 16
C_HEAD_DIM = 64
NA_KH = 8
NA_KW = 16
D_FF = 4 * D_MODEL
M_WIDTH = M_HEADS * M_DV
B_WIDTH = B_HEADS * B_V

LANES = 128
NEG = -1e30
VMEM_LIMIT = 56 * 1024 * 1024

P_Q, P_K, P_V, P_O = 0, 512, 1024, 1536
P_CQ, P_CKV, P_KR1, P_KR2 = 2048, 2304, 2432, 2560
P_MAIN = 2688
P_GATES = 256
MLSTM_CHUNK = 256
NV = 2 * M_DV


def _cparams(sem):
    return pltpu.CompilerParams(dimension_semantics=sem, vmem_limit_bytes=VMEM_LIMIT)


def _rms(x):
    return x * lax.rsqrt(jnp.mean(x * x, axis=-1, keepdims=True) + EPS)


def _dot(a, b):
    return jnp.dot(a, b, preferred_element_type=F32)


def _dot_nt(a, b):
    return lax.dot_general(a, b, (((1,), (1,)), ((), ())), preferred_element_type=F32)


def _ada_kernel(c_ref, w_ref, b_ref, o_ref):
    c = c_ref[...]
    s = c * jax.nn.sigmoid(c)
    o_ref[...] = jnp.dot(s, w_ref[...], precision=lax.Precision.HIGHEST,
                         preferred_element_type=F32) + b_ref[...]


def _ada(c_rows, ada_w, ada_b):
    depth, d, n = ada_w.shape
    tn = 1536
    return pl.pallas_call(
        _ada_kernel,
        grid=(depth, n // tn),
        in_specs=[pl.BlockSpec((8, d), lambda l, j: (0, 0)),
                  pl.BlockSpec((None, d, tn), lambda l, j: (l, 0, j)),
                  pl.BlockSpec((None, 1, tn), lambda l, j: (l, 0, j))],
        out_specs=pl.BlockSpec((None, 8, tn), lambda l, j: (l, 0, j)),
        out_shape=jax.ShapeDtypeStruct((depth, 8, n), F32),
        compiler_params=_cparams(("arbitrary", "arbitrary")),
        name="ada",
    )(c_rows, ada_w, ada_b.reshape(depth, 1, n))


def _modproj_kernel(x_ref, sh_ref, sc_ref, g_ref, w_ref, *o_refs, widths, chunk):
    x = x_ref[...]
    h = _rms(x) * g_ref[...] * (1.0 + sc_ref[...]) + sh_ref[...]
    hb = h.astype(BF16)
    off = 0
    for o_ref, n in zip(o_refs, widths):
        for c0 in range(0, n, chunk):
            c1 = min(n, c0 + chunk)
            o_ref[:, c0:c1] = _dot(hb, w_ref[:, off + c0:off + c1]).astype(o_ref.dtype)
        off += n


def _modproj(x, shift, scale, g, w, widths, dtypes, tm):
    b, t, d = x.shape
    bm = shift.shape[0]
    mod_map = (lambda i, j: (i, 0, 0)) if bm == b else (lambda i, j: (0, 0, 0))
    tm = min(tm, t)
    return pl.pallas_call(
        functools.partial(_modproj_kernel, widths=widths, chunk=512),
        grid=(b, t // tm),
        in_specs=[pl.BlockSpec((None, tm, d), lambda i, j: (i, j, 0)),
                  pl.BlockSpec((None, 1, d), mod_map),
                  pl.BlockSpec((None, 1, d), mod_map),
                  pl.BlockSpec((1, d), lambda i, j: (0, 0)),
                  pl.BlockSpec(w.shape, lambda i, j: (0, 0))],
        out_specs=[pl.BlockSpec((None, tm, n), lambda i, j: (i, j, 0)) for n in widths],
        out_shape=[jax.ShapeDtypeStruct((b, t, n), dt) for n, dt in zip(widths, dtypes)],
        compiler_params=_cparams(("parallel", "parallel")),
        name="modproj",
    )(x, shift, scale, g.reshape(1, d), w)


def _mlstm_kernel(*refs, has_init, emit_state):
    q_ref, k_ref, v_ref, g_ref, gb_ref = refs[:5]
    pos = 5
    if has_init:
        c0_ref, m0_ref = refs[pos:pos + 2]
        pos += 2
    h_ref = refs[pos]
    pos += 1
    if emit_state:
        cout_ref, mout_ref = refs[pos:pos + 2]
        pos += 2
    c_s, m_s = refs[pos:pos + 2]

    d = pl.program_id(1)
    j = pl.program_id(2)
    L = q_ref.shape[0]

    @pl.when(j == 0)
    def _():
        if has_init:
            c_s[...] = c0_ref[...]
            m_s[...] = m0_ref[...]
        else:
            c_s[...] = jnp.zeros_like(c_s)
            m_s[...] = jnp.zeros_like(m_s)

    r = lax.broadcasted_iota(jnp.int32, (L, L), 0)
    c = lax.broadcasted_iota(jnp.int32, (L, L), 1)
    sgn = 1 - 2 * d
    keep = (r - c) * sgn >= 0
    tri = keep.astype(F32)

    gates = g_ref[...] + gb_ref[...]
    ls = jax.nn.log_sigmoid(gates)
    gates_t = gates.T[0:8]
    ls_t = jax.nn.log_sigmoid(gates_t)
    hi = lax.Precision.HIGHEST
    b_cols = jnp.dot(tri, ls, precision=hi, preferred_element_type=F32)
    b_rows = lax.dot_general(ls_t, tri, (((1,), (1,)), ((), ())), precision=hi,
                             preferred_element_type=F32)
    b_tot = jnp.sum(ls, axis=0, keepdims=True)

    lane = lax.broadcasted_iota(jnp.int32, (L, M_DV), 1)
    ones_col = (lane == 0).astype(BF16)
    scale = M_DK ** -0.5

    for h in range(M_HEADS):
        sl = slice(h * M_DK, (h + 1) * M_DK)
        q = q_ref[:, sl]
        k = k_ref[:, sl]
        v = v_ref[:, sl]
        vext = jnp.concatenate([v, ones_col], axis=-1)
        cn = c_s[h]
        m = m_s[h][0:1, 0:1]

        li_c = gates[:, h:h + 1]
        b_c = b_cols[:, 4 + h:5 + h]
        li_r = gates_t[h:h + 1, :]
        b_r = b_rows[4 + h:5 + h, :]
        b_last = b_tot[:, 4 + h:5 + h]

        dm = jnp.where(keep, b_c - b_r + li_r, NEG)
        inter = b_c + m
        m_t = jnp.maximum(jnp.max(dm, axis=-1, keepdims=True), inter)
        s = _dot_nt(q, k) * (scale * jnp.exp(dm - m_t))
        w_inter = jnp.exp(inter - m_t) * scale
        nd = w_inter * _dot(q, cn.astype(BF16)) + _dot(s.astype(BF16), vext)
        den = nd[:, M_DV:M_DV + 1]
        hh = nd[:, :M_DV] / jnp.maximum(jnp.abs(den), jnp.exp(-m_t))
        h_ref[:, sl] = hh

        g_c = b_last - b_c + li_c
        m_new = jnp.maximum(b_last + m, jnp.max(g_c, axis=0, keepdims=True))
        w_k = jnp.exp(g_c - m_new)
        decay = jnp.exp(b_last + m - m_new)
        kw = (k.astype(F32) * w_k).astype(BF16)
        upd = lax.dot_general(kw, vext, (((0,), (0,)), ((), ())), preferred_element_type=F32)
        c_s[h] = decay * cn + upd
        m_s[h] = jnp.broadcast_to(m_new, m_s.shape[1:])

    if emit_state:
        cout_ref[...] = c_s[...]
        mout_ref[...] = m_s[...]


def _mlstm(p_main, p_gates, gate_b2, init, emit_state):
    b, t, _ = p_main.shape
    L = min(MLSTM_CHUNK, t)
    nc = t // L

    def cidx(d, j):
        return j + d * (nc - 1 - 2 * j)

    def col(cb):
        return pl.BlockSpec((None, L, M_WIDTH), lambda i, d, j: (i, cidx(d, j), cb))

    in_specs = [col(P_Q // M_WIDTH), col(P_K // M_WIDTH), col(P_V // M_WIDTH),
                pl.BlockSpec((None, L, LANES), lambda i, d, j: (i, cidx(d, j), d)),
                pl.BlockSpec((None, 1, LANES), lambda i, d, j: (d, 0, 0))]
    args = [p_main, p_main, p_main, p_gates, gate_b2]
    st_c = pl.BlockSpec((None, None, M_HEADS, M_DK, NV), lambda i, d, j: (i, d, 0, 0, 0))
    st_m = pl.BlockSpec((None, None, M_HEADS, 8, LANES), lambda i, d, j: (i, d, 0, 0, 0))
    if init is not None:
        in_specs += [st_c, st_m]
        args += list(init)
    out_specs = [pl.BlockSpec((None, None, L, M_WIDTH), lambda i, d, j: (d, i, cidx(d, j), 0))]
    out_shape = [jax.ShapeDtypeStruct((2, b, t, M_WIDTH), F32)]
    if emit_state:
        out_specs += [st_c, st_m]
        out_shape += [jax.ShapeDtypeStruct((b, 2, M_HEADS, M_DK, NV), F32),
                      jax.ShapeDtypeStruct((b, 2, M_HEADS, 8, LANES), F32)]
    return pl.pallas_call(
        functools.partial(_mlstm_kernel, has_init=init is not None, emit_state=emit_state),
        grid=(b, 2, nc),
        in_specs=in_specs,
        out_specs=out_specs,
        out_shape=out_shape,
        scratch_shapes=[pltpu.VMEM((M_HEADS, M_DK, NV), F32),
                        pltpu.VMEM((M_HEADS, 8, LANES), F32)],
        compiler_params=_cparams(("parallel", "parallel", "arbitrary")),
        name="mlstm",
    )(*args)


def _mlaprep_kernel(cq_ref, ckv_ref, kr1_ref, kr2_ref, cos_ref, sin_ref, cost_ref, sint_ref,
                    qg_ref, kvg_ref, wq1_ref, wq2_ref, wk_ref, wv_ref, qt_ref, k_ref, vt_ref):
    cos = cos_ref[...]
    sin = sin_ref[...]
    cq = cq_ref[...].astype(F32)
    cqn = (_rms(cq) * qg_ref[...]).astype(BF16)
    ckv = ckv_ref[...].astype(F32)
    ckvn = (_rms(ckv) * kvg_ref[...]).astype(BF16)
    a_scale = (B_NOPE + B_ROPE) ** -0.5 * float(np.log2(np.e))
    kr = kr1_ref[...].astype(F32) * cos + kr2_ref[...].astype(F32) * sin
    cos_t = cost_ref[...] * a_scale
    sin_t = sint_ref[...] * a_scale
    for c0 in range(0, B_WIDTH, LANES):
        vt_ref[c0:c0 + LANES, :] = _dot_nt(wv_ref[c0:c0 + LANES, :], ckvn).astype(vt_ref.dtype)
    for h in range(B_HEADS):
        sl = slice(h * LANES, (h + 1) * LANES)
        q1 = _dot_nt(wq1_ref[sl, :], cqn)
        q2 = _dot_nt(wq2_ref[sl, :], cqn)
        qt_ref[sl, :] = (q1 * cos_t + q2 * sin_t).astype(qt_ref.dtype)
        k_ref[:, sl] = (_dot(ckvn, wk_ref[:, sl]) + kr).astype(k_ref.dtype)


def _mlaprep(p_main, tables, qg, kvg, wq1, wq2, wk, wv, tm):
    cos, sin, cos_t, sin_t = tables
    b, t, _ = p_main.shape
    tm = min(tm, t)
    hw = B_HEADS * LANES

    def tok(width, cb):
        return pl.BlockSpec((None, tm, width), lambda i, j: (i, j, cb))

    def full(a):
        return pl.BlockSpec(a.shape, lambda i, j: (0,) * a.ndim)

    return pl.pallas_call(
        _mlaprep_kernel,
        grid=(b, t // tm),
        in_specs=[tok(B_Q_RANK, P_CQ // B_Q_RANK), tok(LANES, P_CKV // LANES),
                  tok(LANES, P_KR1 // LANES), tok(LANES, P_KR2 // LANES),
                  pl.BlockSpec((tm, LANES), lambda i, j: (j, 0)),
                  pl.BlockSpec((tm, LANES), lambda i, j: (j, 0)),
                  pl.BlockSpec((LANES, tm), lambda i, j: (0, j)),
                  pl.BlockSpec((LANES, tm), lambda i, j: (0, j)),
                  full(qg), full(kvg), full(wq1), full(wq2), full(wk), full(wv)],
        out_specs=[pl.BlockSpec((None, hw, tm), lambda i, j: (i, 0, j)),
                   pl.BlockSpec((None, tm, hw), lambda i, j: (i, j, 0)),
                   pl.BlockSpec((None, B_WIDTH, tm), lambda i, j: (i, 0, j))],
        out_shape=[jax.ShapeDtypeStruct((b, hw, t), BF16),
                   jax.ShapeDtypeStruct((b, t, hw), BF16),
                   jax.ShapeDtypeStruct((b, B_WIDTH, t), BF16)],
        compiler_params=_cparams(("parallel", "parallel")),
        name="mlaprep",
    )(p_main, p_main, p_main, p_main, cos, sin, cos_t, sin_t, qg, kvg, wq1, wq2, wk, wv)


def _flash_kernel(*refs, tk, nk, has_ctx):
    if has_ctx:
        q_ref, k_ref, v_ref, kc_ref, vc_ref, o_ref, s_s, p_s = refs
    else:
        q_ref, k_ref, v_ref, o_ref, s_s, p_s = refs
    tq = q_ref.shape[1]
    sls = [slice(h * LANES, (h + 1) * LANES) for h in range(2)]
    vsl = [slice(h * B_V, (h + 1) * B_V) for h in range(2)]

    def stage_a(h, slot, kb):
        s = _dot(kb, q_ref[sls[h], :])
        s_s[h, slot, 0:s.shape[0]] = s
        return jnp.max(s, axis=0, keepdims=True)

    def stage_b(h, slot, rows, mx, m, l):
        m_new = jnp.maximum(m, mx)
        alpha = jnp.exp2(m - m_new)
        p = jnp.exp2(s_s[h, slot, 0:rows] - m_new)
        p_s[h, slot, 0:rows] = p.astype(BF16)
        return m_new, alpha * l + jnp.sum(p, axis=0, keepdims=True), alpha

    def stage_c(h, slot, rows, vb, alpha, acc):
        return alpha * acc + _dot(vb, p_s[h, slot, 0:rows])

    def kblock(j, h):
        return k_ref[pl.ds(pl.multiple_of(j * tk, tk), tk), sls[h]]

    def vblock(j, h):
        return v_ref[vsl[h], pl.ds(pl.multiple_of(j * tk, tk), tk)]

    def trip(j, slot, state):
        new = []
        for h in range(2):
            mx, m, l, alpha, acc = state[h]
            mx_new = stage_a(h, slot, kblock(j, h))
            m, l, alpha_new = stage_b(h, 1 - slot, tk, mx, m, l)
            acc = stage_c(h, slot, tk, vblock(j - 2, h), alpha, acc)
            new.append((mx_new, m, l, alpha_new, acc))
        return tuple(new)

    zrow = jnp.zeros((1, tq), F32)
    neg = jnp.full((1, tq), NEG, F32)
    zacc = jnp.zeros((B_V, tq), F32)
    state = []
    for h in range(2):
        mx = stage_a(h, 0, k_ref[0:tk, sls[h]])
        if nk > 1:
            mx1 = stage_a(h, 1, k_ref[tk:2 * tk, sls[h]])
            m, l, alpha = stage_b(h, 0, tk, mx, neg, zrow)
            state.append((mx1, m, l, alpha, zacc))
        else:
            state.append((mx, neg, zrow, zrow, zacc))
    state = tuple(state)
    if nk > 2:
        assert nk % 2 == 0

        def body(i, state):
            return trip(2 * i + 1, 1, trip(2 * i, 0, state))

        state = lax.fori_loop(1, nk // 2, body, state)
    outs = []
    for h in range(2):
        mx, m, l, alpha, acc = state[h]
        last = (nk - 1) % 2
        if nk > 1:
            acc = stage_c(h, 1 - last, tk, v_ref[vsl[h], (nk - 2) * tk:(nk - 1) * tk], alpha, acc)
        m, l, alpha = stage_b(h, last, tk, mx, m, l)
        acc = stage_c(h, last, tk, v_ref[vsl[h], (nk - 1) * tk:nk * tk], alpha, acc)
        if has_ctx:
            tc = kc_ref.shape[0]
            mx = stage_a(h, 1 - last, kc_ref[:, sls[h]])
            m, l, alpha = stage_b(h, 1 - last, tc, mx, m, l)
            acc = stage_c(h, 1 - last, tc, vc_ref[vsl[h], :], alpha, acc)
        outs.append(acc / l)
    o_ref[...] = jnp.concatenate(outs, axis=0).T.astype(o_ref.dtype)


def _flash(q_t, k, v_t, kc, vc_t, tq, tk):
    b, _, t = q_t.shape
    tkeys = k.shape[1]
    tq = min(tq, t)
    tk = min(tk, tkeys)
    has_ctx = kc is not None
    in_specs = [pl.BlockSpec((None, 2 * LANES, tq), lambda i, hp, j: (i, hp, j)),
                pl.BlockSpec((None, tkeys, 2 * LANES), lambda i, hp, j: (i, 0, hp)),
                pl.BlockSpec((None, LANES, tkeys), lambda i, hp, j: (i, hp, 0))]
    args = [q_t, k, v_t]
    if has_ctx:
        tc = kc.shape[1]
        assert tc <= tk
        in_specs += [pl.BlockSpec((None, tc, 2 * LANES), lambda i, hp, j: (i, 0, hp)),
                     pl.BlockSpec((None, LANES, tc), lambda i, hp, j: (i, hp, 0))]
        args += [kc, vc_t]
    return pl.pallas_call(
        functools.partial(_flash_kernel, tk=tk, nk=tkeys // tk, has_ctx=has_ctx),
        grid=(b, B_HEADS // 2, t // tq),
        in_specs=in_specs,
        out_specs=pl.BlockSpec((None, tq, LANES), lambda i, hp, j: (i, j, hp)),
        out_shape=jax.ShapeDtypeStruct((b, t, B_WIDTH), BF16),
        scratch_shapes=[pltpu.VMEM((2, 2, tk, tq), F32), pltpu.VMEM((2, 2, tk, tq), BF16)],
        compiler_params=_cparams(("parallel", "parallel", "arbitrary")),
        name="flash",
    )(*args)


def _natten_kernel(q_ref, k_ref, v_ref, kc_ref, vc_ref, bias_ref, o_ref, *, rb, rows):
    blk = pl.program_id(2)
    kc = kc_ref[...]
    vc = vc_ref[...]
    tq = rb * GRID_W
    lane = lax.broadcasted_iota(jnp.int32, (tq, LANES), 1)
    scale = C_HEAD_DIM ** -0.5
    win = NA_KH * GRID_W
    q_all = q_ref[...]
    heads = []
    for h in range(2):
        in_head = (lane >= h * C_HEAD_DIM) & (lane < (h + 1) * C_HEAD_DIM)
        qh = jnp.where(in_head, q_all, jnp.zeros_like(q_all))
        s_c = _dot_nt(qh, kc) * scale
        m_c = jnp.max(s_c, axis=-1, keepdims=True)
        koffs, s_ws = [], []
        for i in range(rb):
            r = blk * rb + i
            rs = jnp.clip(r - NA_KH // 2, 0, rows - NA_KH)
            koff = pl.multiple_of(rs * GRID_W, GRID_W)
            koffs.append(koff)
            s_ws.append(_dot_nt(qh[i * GRID_W:(i + 1) * GRID_W], k_ref[pl.ds(koff, win), :]) * scale
                        + bias_ref[h, r - rs])
        p_ws, p_c, l_all = [], [], []
        for i in range(rb):
            rsl = slice(i * GRID_W, (i + 1) * GRID_W)
            m = jnp.maximum(jnp.max(s_ws[i], axis=-1, keepdims=True), m_c[rsl])
            p_w = jnp.exp(s_ws[i] - m)
            p_ci = jnp.exp(s_c[rsl] - m)
            l_all.append(jnp.sum(p_w, axis=-1, keepdims=True) + jnp.sum(p_ci, axis=-1, keepdims=True))
            p_ws.append(p_w.astype(BF16))
            p_c.append(p_ci.astype(BF16))
        o_w = [_dot(p_ws[i], v_ref[pl.ds(koffs[i], win), :]) for i in range(rb)]
        o = jnp.concatenate(o_w, axis=0) + _dot(jnp.concatenate(p_c, axis=0), vc)
        heads.append(o / jnp.concatenate(l_all, axis=0))
    o_ref[...] = jnp.where(lane < C_HEAD_DIM, heads[0], heads[1]).astype(o_ref.dtype)


def _natten(qkv, qkv_c, bias_t, rb):
    b, t, _ = qkv.shape
    tc = qkv_c.shape[1]
    rows = t // GRID_W
    rb = min(rb, rows)
    nhp = C_HEADS // 2
    return pl.pallas_call(
        functools.partial(_natten_kernel, rb=rb, rows=rows),
        grid=(b, nhp, rows // rb),
        in_specs=[pl.BlockSpec((None, rb * GRID_W, LANES), lambda i, hp, j: (i, j, hp)),
                  pl.BlockSpec((None, t, LANES), lambda i, hp, j: (i, 0, nhp + hp)),
                  pl.BlockSpec((None, t, LANES), lambda i, hp, j: (i, 0, 2 * nhp + hp)),
                  pl.BlockSpec((None, tc, LANES), lambda i, hp, j: (i, 0, nhp + hp)),
                  pl.BlockSpec((None, tc, LANES), lambda i, hp, j: (i, 0, 2 * nhp + hp)),
                  pl.BlockSpec((None,) + bias_t.shape[1:], lambda i, hp, j: (hp, 0, 0, 0, 0))],
        out_specs=pl.BlockSpec((None, rb * GRID_W, LANES), lambda i, hp, j: (i, j, hp)),
        out_shape=jax.ShapeDtypeStruct((b, t, C_HEADS * C_HEAD_DIM), BF16),
        compiler_params=_cparams(("parallel", "parallel", "arbitrary")),
        name="natten",
    )(qkv, qkv, qkv, qkv_c, qkv_c, bias_t)


def _natten_bias_table(rel_bias):
    qc = np.arange(GRID_W)[:, None]
    kcol = np.arange(GRID_W)[None, :]
    cs = np.clip(qc - NA_KW // 2, 0, GRID_W - NA_KW)
    valid = (kcol >= cs) & (kcol < cs + NA_KW)
    pad = GRID_W - NA_KW
    rbp = jnp.pad(rel_bias.astype(F32), ((0, 0), (0, 0), (pad, pad)))
    toe = jnp.stack([rbp[:, :, GRID_W - 1 - q:2 * GRID_W - 1 - q] for q in range(GRID_W)], axis=2)
    toe = jnp.where(valid[None, None], toe, NEG)
    tab = jnp.stack([toe[:, NA_KH - 1 - var:2 * NA_KH - 1 - var] for var in range(NA_KH)], axis=1)
    return tab.transpose(0, 1, 3, 2, 4).reshape(C_HEADS // 2, 2, NA_KH, GRID_W, NA_KH * GRID_W)


def _mlp_tail(x1, g2, sh2, sc2, n2g, w1_ref, w2_ref):
    h = (_rms(x1) * n2g * (1.0 + sc2) + sh2).astype(BF16)
    acc = jnp.zeros_like(x1)
    chunk = 1024
    for c0 in range(0, D_FF, chunk):
        a = jnp.maximum(_dot(h, w1_ref[:, c0:c0 + chunk]), 0.0)
        acc = acc + _dot((a * a).astype(BF16), w2_ref[c0:c0 + chunk, :])
    return x1 + g2 * acc


def _post_ab_kernel(x_ref, hf_ref, hb_ref, og_ref, bl_ref, mg_ref, g1_ref, sh2_ref, sc2_ref,
                    g2_ref, n2g_ref, wo_ref, w1_ref, w2_ref, o_ref):
    hm = hf_ref[...] + hb_ref[...]
    og = jax.nn.sigmoid(og_ref[...].astype(F32))
    mg = mg_ref[...]
    y = jnp.zeros(x_ref.shape, F32)
    for h in range(M_HEADS):
        sl = slice(h * M_DV, (h + 1) * M_DV)
        hn = (_rms(hm[:, sl]) * mg[:, sl] * og[:, sl]).astype(BF16)
        y = y + _dot(hn, wo_ref[sl, :])
    y = y + _dot(bl_ref[...], wo_ref[M_WIDTH:, :])
    x1 = x_ref[...] + g1_ref[...] * y
    o_ref[...] = _mlp_tail(x1, g2_ref[...], sh2_ref[...], sc2_ref[...], n2g_ref[...],
                           w1_ref, w2_ref)


def _post_c_kernel(x_ref, y_ref, g1_ref, sh2_ref, sc2_ref, g2_ref, n2g_ref, fg_ref,
                   wo_ref, w1_ref, w2_ref, o_ref):
    x1 = x_ref[...] + g1_ref[...] * _dot(y_ref[...], wo_ref[...])
    x2 = _mlp_tail(x1, g2_ref[...], sh2_ref[...], sc2_ref[...], n2g_ref[...], w1_ref, w2_ref)
    o_ref[...] = _rms(x2) * fg_ref[...]


def _resident(a):
    return pl.BlockSpec(a.shape, lambda i, j: (0,) * a.ndim, pipeline_mode=pl.Buffered(1))


def _post_ab(x, h_dirs, p_main, b_lat, mg, mods, n2g, wo, w1, w2, tm):
    b, t, d = x.shape
    tm = min(tm, t)
    g1, sh2, sc2, g2 = mods
    bm = g1.shape[0]
    mod_map = (lambda i, j: (i, 0, 0)) if bm == b else (lambda i, j: (0, 0, 0))
    mod_spec = pl.BlockSpec((None, 1, d), mod_map)
    tok = lambda w, cb: pl.BlockSpec((None, tm, w), lambda i, j: (i, j, cb))
    return pl.pallas_call(
        _post_ab_kernel,
        grid=(b, t // tm),
        in_specs=[tok(d, 0),
                  pl.BlockSpec((None, None, tm, M_WIDTH), lambda i, j: (0, i, j, 0)),
                  pl.BlockSpec((None, None, tm, M_WIDTH), lambda i, j: (1, i, j, 0)),
                  tok(M_WIDTH, P_O // M_WIDTH), tok(B_WIDTH, 0),
                  pl.BlockSpec((1, M_WIDTH), lambda i, j: (0, 0)),
                  mod_spec, mod_spec, mod_spec, mod_spec,
                  pl.BlockSpec((1, d), lambda i, j: (0, 0)),
                  _resident(wo), _resident(w1), _resident(w2)],
        out_specs=tok(d, 0),
        out_shape=jax.ShapeDtypeStruct((b, t, d), F32),
        compiler_params=_cparams(("parallel", "parallel")),
        name="post_ab",
    )(x, h_dirs, h_dirs, p_main, b_lat, mg, g1, sh2, sc2, g2, n2g.reshape(1, d), wo, w1, w2)


def _post_c(x, y, mods, n2g, fg, wo, w1, w2, tm):
    b, t, d = x.shape
    tm = min(tm, t)
    g1, sh2, sc2, g2 = mods
    mod_spec = pl.BlockSpec((None, 1, d), lambda i, j: (i, 0, 0))
    tok = pl.BlockSpec((None, tm, d), lambda i, j: (i, j, 0))
    vec = pl.BlockSpec((1, d), lambda i, j: (0, 0))
    return pl.pallas_call(
        _post_c_kernel,
        grid=(b, t // tm),
        in_specs=[tok, tok, mod_spec, mod_spec, mod_spec, mod_spec, vec, vec,
                  _resident(wo), _resident(w1), _resident(w2)],
        out_specs=tok,
        out_shape=jax.ShapeDtypeStruct((b, t, d), F32),
        compiler_params=_cparams(("parallel", "parallel")),
        name="post_c",
    )(x, y, g1, sh2, sc2, g2, n2g.reshape(1, d), fg.reshape(1, d), wo, w1, w2)


def _rope_tables(t):
    pos = np.arange(t)
    row = (pos // GRID_W).astype(np.float32)
    colp = (pos % GRID_W).astype(np.float32)
    n_f = B_ROPE // 4
    freqs = (ROPE_BASE ** (-np.arange(n_f, dtype=np.float32) / n_f)).astype(np.float32)
    ang = np.concatenate([row[:, None] * freqs, colp[:, None] * freqs], axis=-1)
    cos = np.zeros((t, LANES), np.float32)
    sin = np.zeros((t, LANES), np.float32)
    cos[:, :B_NOPE] = 1.0
    cos[:, B_NOPE:B_NOPE + 16] = np.cos(ang)
    cos[:, B_NOPE + 16:B_NOPE + 32] = np.cos(ang)
    sin[:, B_NOPE:B_NOPE + 16] = np.sin(ang)
    sin[:, B_NOPE + 16:B_NOPE + 32] = np.sin(ang)
    return tuple(jnp.asarray(a) for a in (cos, sin, cos.T.copy(), sin.T.copy()))


def _identity_tables(t):
    cos = np.zeros((t, LANES), np.float32)
    cos[:, :B_NOPE + B_ROPE] = 1.0
    sin = np.zeros((t, LANES), np.float32)
    return tuple(jnp.asarray(a) for a in (cos, sin, cos.T.copy(), sin.T.copy()))


def _layer0_weights(w_in, gate_b, w_uq, w_ukv):
    d = w_in.shape[0]
    half = B_ROPE // 2
    o_g = 4 * M_WIDTH
    o_cq = o_g + 4 * M_HEADS
    o_ckv = o_cq + B_Q_RANK
    o_kr = o_ckv + B_KV_RANK
    kr = w_in[:, o_kr:o_kr + B_ROPE]
    z = lambda n: jnp.zeros((d, n), w_in.dtype)
    kr1 = jnp.concatenate([z(B_NOPE), kr, z(LANES - B_NOPE - B_ROPE)], axis=1)
    kr2 = jnp.concatenate([z(B_NOPE), -kr[:, half:], kr[:, :half], z(LANES - B_NOPE - B_ROPE)], axis=1)
    wg = w_in[:, o_g:o_g + 4 * M_HEADS]
    gpad = z(LANES - 2 * M_HEADS)
    w0 = jnp.concatenate([w_in[:, :o_g], w_in[:, o_cq:o_kr], kr1, kr2,
                          wg[:, :2 * M_HEADS], gpad, wg[:, 2 * M_HEADS:], gpad], axis=1).astype(BF16)
    gb = jnp.zeros((2, 1, LANES), F32)
    gb = gb.at[0, 0, :2 * M_HEADS].set(gate_b[:2 * M_HEADS]).at[1, 0, :2 * M_HEADS].set(gate_b[2 * M_HEADS:])

    qr = B_Q_RANK
    wq = w_uq.reshape(qr, B_HEADS, B_NOPE + B_ROPE)
    nope, ra, rb = wq[..., :B_NOPE], wq[..., B_NOPE:B_NOPE + half], wq[..., B_NOPE + half:]
    zq = jnp.zeros((qr, B_HEADS, LANES - B_NOPE - B_ROPE), w_uq.dtype)
    wq1 = jnp.concatenate([nope, ra, rb, zq], axis=-1).reshape(qr, B_HEADS * LANES).astype(BF16)
    wq2 = jnp.concatenate([jnp.zeros_like(nope), -rb, ra, zq], axis=-1).reshape(qr, B_HEADS * LANES).astype(BF16)
    wkv = w_ukv.reshape(B_KV_RANK, B_HEADS, B_NOPE + B_V)
    wk = jnp.concatenate([wkv[..., :B_NOPE], jnp.zeros((B_KV_RANK, B_HEADS, LANES - B_NOPE), w_ukv.dtype)],
                         axis=-1).reshape(B_KV_RANK, B_HEADS * LANES).astype(BF16)
    wv = wkv[..., B_NOPE:].reshape(B_KV_RANK, B_WIDTH).astype(BF16)
    return w0, gb, wq1.T, wq2.T, wk, wv.T


def _split_mod(mod_rows):
    return [m[:, None, :] for m in jnp.split(mod_rows, 6, axis=-1)]


def kernel(x, c, ctx, c_ctx, ada_w, ada_b, norm1_g, norm2_g, mlp_w1, mlp_w2, ab_w_in, ab_gate_b,
           ab_m_norm_g, ab_q_norm_g, ab_kv_norm_g, ab_w_uq, ab_w_ukv, ab_w_out, na_w_in,
           na_rel_bias, na_w_out, final_norm_g):
    b, t, d = x.shape
    tc = ctx.shape[1]
    c_rows = jnp.concatenate([c, c_ctx[None, :], jnp.zeros((8 - b - 1, d), F32)], axis=0)
    mod = _ada(c_rows, ada_w, ada_b)

    sh1, sc1, g1, sh2, sc2, g2 = _split_mod(mod[0, :b])
    csh1, csc1, cg1, csh2, csc2, cg2 = _split_mod(mod[0, b:b + 1])
    w0, gb, wq1, wq2, wk, wv = _layer0_weights(ab_w_in[0], ab_gate_b[0], ab_w_uq[0], ab_w_ukv[0])
    widths, dtypes = (P_MAIN, P_GATES), (BF16, F32)
    p_main, p_gates = _modproj(x, sh1, sc1, norm1_g[0], w0, widths, dtypes, tm=512)
    pc_main, pc_gates = _modproj(ctx, csh1, csc1, norm1_g[0], w0, widths, dtypes, tm=256)

    hc_dirs, st_c, st_m = _mlstm(pc_main, pc_gates, gb, None, True)
    (h_dirs,) = _mlstm(p_main, p_gates, gb, (st_c, st_m), False)

    qg = ab_q_norm_g[0].reshape(1, -1)
    kvg = ab_kv_norm_g[0].reshape(1, -1)
    q_l, k_l, v_l = _mlaprep(p_main, _rope_tables(t), qg, kvg, wq1, wq2, wk, wv, tm=512)
    q_c, k_c, v_c = _mlaprep(pc_main, _identity_tables(tc), qg, kvg, wq1, wq2, wk, wv, tm=256)
    b_lat = _flash(q_l, k_l, v_l, k_c, v_c, tq=512, tk=512)
    b_ctx = _flash(q_c, k_c, v_c, None, None, tq=256, tk=256)

    wo = ab_w_out[0].astype(BF16)
    w1 = mlp_w1[0].astype(BF16)
    w2 = mlp_w2[0].astype(BF16)
    mg = ab_m_norm_g[0].reshape(1, M_WIDTH)
    x = _post_ab(x, h_dirs, p_main, b_lat, mg, (g1, sh2, sc2, g2), norm2_g[0], wo, w1, w2, tm=512)
    ctx = _post_ab(ctx, hc_dirs, pc_main, b_ctx, mg, (cg1, csh2, csc2, cg2), norm2_g[0],
                   wo, w1, w2, tm=256)

    sh1, sc1, g1, sh2, sc2, g2 = _split_mod(mod[1, :b])
    csh1, csc1 = _split_mod(mod[1, b:b + 1])[:2]
    wn = na_w_in[0].astype(BF16)
    nw = (wn.shape[1],)
    (qkv,) = _modproj(x, sh1, sc1, norm1_g[1], wn, nw, (BF16,), tm=512)
    (qkv_c,) = _modproj(ctx, csh1, csc1, norm1_g[1], wn, nw, (BF16,), tm=256)
    bias_t = _natten_bias_table(na_rel_bias[0])
    y = _natten(qkv, qkv_c, bias_t, rb=8)
    return _post_c(x, y, (g1, sh2, sc2, g2), norm2_g[1], final_norm_g,
                   na_w_out[0].astype(BF16), mlp_w1[1].astype(BF16), mlp_w2[1].astype(BF16), tm=512)
```

```python
import functools

import numpy as np
import jax
import jax.numpy as jnp
from jax import lax
from jax.experimental import pallas as pl
from jax.experimental.pallas import tpu as pltpu

F32 = jnp.float32
BF16 = jnp.bfloat16

D_MODEL = 1024
DEPTH = 2
GRID_W = 64
EPS = 1e-6
M_HEADS = 4
M_DK = 128
M_DV = 128
B_HEADS = 8
B_Q_RANK = 256
B_KV_RANK = 128
B_NOPE = 64
B_ROPE = 32
B_V = 64
ROPE_BASE = 10000.0
C_HEADS = 16
C_HEAD_DIM = 64
NA_KH = 8
NA_KW = 16
D_FF = 4 * D_MODEL
M_WIDTH = M_HEADS * M_DV
B_WIDTH = B_HEADS * B_V

LANES = 128
NEG = -1e30
LOG2E = float(np.log2(np.e))
VMEM_LIMIT = 56 * 1024 * 1024

P_Q, P_K, P_V, P_O = 0, 512, 1024, 1536
P_CQ, P_CKV, P_KR1, P_KR2 = 2048, 2304, 2432, 2560
P_MAIN = 2688
P_GATES = 256
MLSTM_CHUNK = 256
NV = 2 * M_DV
BF16_ROWS = 16
V_EXT = B_V + BF16_ROWS


def _cparams(sem):
    return pltpu.CompilerParams(dimension_semantics=sem, vmem_limit_bytes=VMEM_LIMIT)


def _rms(x):
    return x * lax.rsqrt(jnp.mean(x * x, axis=-1, keepdims=True) + EPS)


def _dot(a, b):
    return jnp.dot(a, b, preferred_element_type=F32)


def _dot_nt(a, b):
    return lax.dot_general(a, b, (((1,), (1,)), ((), ())), preferred_element_type=F32)


def _ada_kernel(c_ref, w_ref, b_ref, o_ref):
    c = c_ref[...]
    s = c * jax.nn.sigmoid(c)
    o_ref[...] = jnp.dot(s, w_ref[...], precision=lax.Precision.HIGHEST,
                         preferred_element_type=F32) + b_ref[...]


def _ada(c_rows, ada_w, ada_b):
    depth, d, n = ada_w.shape
    tn = 1536
    return pl.pallas_call(
        _ada_kernel,
        grid=(depth, n // tn),
        in_specs=[pl.BlockSpec((8, d), lambda l, j: (0, 0)),
                  pl.BlockSpec((None, d, tn), lambda l, j: (l, 0, j)),
                  pl.BlockSpec((None, 1, tn), lambda l, j: (l, 0, j))],
        out_specs=pl.BlockSpec((None, 8, tn), lambda l, j: (l, 0, j)),
        out_shape=jax.ShapeDtypeStruct((depth, 8, n), F32),
        compiler_params=_cparams(("arbitrary", "arbitrary")),
        name="ada",
    )(c_rows, ada_w, ada_b.reshape(depth, 1, n))


def _modproj_kernel(x_ref, sh_ref, sc_ref, g_ref, w_ref, *refs, widths, has_t, chunk):
    wt_ref = refs[0] if has_t else None
    o_refs = refs[1:] if has_t else refs
    x = x_ref[...]
    h = _rms(x) * g_ref[...] * (1.0 + sc_ref[...]) + sh_ref[...]
    hb = h.astype(BF16)
    off = 0
    for o_ref, n in zip(o_refs, widths):
        for c0 in range(0, n, chunk):
            c1 = min(n, c0 + chunk)
            o_ref[:, c0:c1] = _dot(hb, w_ref[:, off + c0:off + c1]).astype(o_ref.dtype)
        off += n
    if has_t:
        ot_ref = o_refs[-1]
        for c0 in range(0, wt_ref.shape[0], chunk):
            ot_ref[c0:c0 + chunk, :] = _dot_nt(wt_ref[c0:c0 + chunk, :], hb).astype(ot_ref.dtype)


def _modproj(x, shift, scale, g, w, widths, dtypes, tm, w_t=None):
    b, t, d = x.shape
    bm = shift.shape[0]
    mod_map = (lambda i, j: (i, 0, 0)) if bm == b else (lambda i, j: (0, 0, 0))
    tm = min(tm, t)
    in_specs = [pl.BlockSpec((None, tm, d), lambda i, j: (i, j, 0)),
                pl.BlockSpec((None, 1, d), mod_map),
                pl.BlockSpec((None, 1, d), mod_map),
                pl.BlockSpec((1, d), lambda i, j: (0, 0)),
                pl.BlockSpec(w.shape, lambda i, j: (0, 0))]
    args = [x, shift, scale, g.reshape(1, d), w]
    out_specs = [pl.BlockSpec((None, tm, n), lambda i, j: (i, j, 0)) for n in widths]
    out_shape = [jax.ShapeDtypeStruct((b, t, n), dt) for n, dt in zip(widths, dtypes)]
    if w_t is not None:
        in_specs.append(pl.BlockSpec(w_t.shape, lambda i, j: (0, 0)))
        args.append(w_t)
        out_specs.append(pl.BlockSpec((None, w_t.shape[0], tm), lambda i, j: (i, 0, j)))
        out_shape.append(jax.ShapeDtypeStruct((b, w_t.shape[0], t), BF16))
    return pl.pallas_call(
        functools.partial(_modproj_kernel, widths=widths, has_t=w_t is not None, chunk=512),
        grid=(b, t // tm),
        in_specs=in_specs,
        out_specs=out_specs,
        out_shape=out_shape,
        compiler_params=_cparams(("parallel", "parallel")),
        name="modproj",
    )(*args)


def _mlstm_kernel(*refs, has_init, emit_state):
    q_ref, k_ref, v_ref, g_ref, gb_ref = refs[:5]
    pos = 5
    if has_init:
        c0_ref, m0_ref = refs[pos:pos + 2]
        pos += 2
    h_ref = refs[pos]
    pos += 1
    if emit_state:
        cout_ref, mout_ref = refs[pos:pos + 2]
        pos += 2
    c_s, m_s = refs[pos:pos + 2]

    d = pl.program_id(1)
    j = pl.program_id(2)
    L = q_ref.shape[0]

    @pl.when(j == 0)
    def _():
        if has_init:
            c_s[...] = c0_ref[...]
            m_s[...] = m0_ref[...]
        else:
            c_s[...] = jnp.zeros_like(c_s)
            m_s[...] = jnp.zeros_like(m_s)

    r = lax.broadcasted_iota(jnp.int32, (L, L), 0)
    c = lax.broadcasted_iota(jnp.int32, (L, L), 1)
    sgn = 1 - 2 * d
    keep = (r - c) * sgn >= 0

    gates = g_ref[...] + gb_ref[...]
    ls = jax.nn.log_sigmoid(gates)
    gates_t = gates.T[0:8]
    ls_t = jax.nn.log_sigmoid(gates_t)

    def split3(a):
        hi = a.astype(BF16)
        mid = (a - hi.astype(F32)).astype(BF16)
        lo = (a - hi.astype(F32) - mid.astype(F32)).astype(BF16)
        return hi, mid, lo

    trib = keep.astype(BF16)
    bc3 = _dot(trib, jnp.concatenate(split3(ls), axis=1))
    b_cols = bc3[:, 0:LANES] + bc3[:, LANES:2 * LANES] + bc3[:, 2 * LANES:]
    br3 = _dot_nt(jnp.concatenate(split3(ls_t), axis=0), trib)
    b_rows = br3[0:8] + br3[8:16] + br3[16:24]
    b_tot = jnp.sum(ls, axis=0, keepdims=True)

    lane = lax.broadcasted_iota(jnp.int32, (L, M_DV), 1)
    ones_col = (lane == 0).astype(BF16)
    scale = M_DK ** -0.5
    hsl = [slice(h * M_DK, (h + 1) * M_DK) for h in range(M_HEADS)]

    cn, m, s_raw, qc = [], [], [], []
    for h in range(M_HEADS):
        q = q_ref[:, hsl[h]]
        cn.append(c_s[h])
        m.append(m_s[h][0:1, 0:1])
        s_raw.append(_dot_nt(q, k_ref[:, hsl[h]]))
        qc.append(_dot(q, cn[h].astype(BF16)))

    s_w, w_inter, m_t, kw, decay, m_new = [], [], [], [], [], []
    for h in range(M_HEADS):
        li_c = gates[:, h:h + 1]
        b_c = b_cols[:, 4 + h:5 + h]
        li_r = gates_t[h:h + 1, :]
        b_r = b_rows[4 + h:5 + h, :]
        b_last = b_tot[:, 4 + h:5 + h]
        dm = jnp.where(keep, b_c - b_r + li_r, NEG)
        inter = b_c + m[h]
        m_t.append(jnp.maximum(jnp.max(dm, axis=-1, keepdims=True), inter))
        s_w.append((s_raw[h] * (scale * jnp.exp(dm - m_t[h]))).astype(BF16))
        w_inter.append(jnp.exp(inter - m_t[h]) * scale)
        g_c = b_last - b_c + li_c
        m_new.append(jnp.maximum(b_last + m[h], jnp.max(g_c, axis=0, keepdims=True)))
        decay.append(jnp.exp(b_last + m[h] - m_new[h]))
        kw.append((k_ref[:, hsl[h]].astype(F32) * jnp.exp(g_c - m_new[h])).astype(BF16))

    for h in range(M_HEADS):
        vext = jnp.concatenate([v_ref[:, hsl[h]], ones_col], axis=-1)
        nd = w_inter[h] * qc[h] + _dot(s_w[h], vext)
        den = nd[:, M_DV:M_DV + 1]
        h_ref[:, hsl[h]] = nd[:, :M_DV] / jnp.maximum(jnp.abs(den), jnp.exp(-m_t[h]))
        upd = lax.dot_general(kw[h], vext, (((0,), (0,)), ((), ())), preferred_element_type=F32)
        c_s[h] = decay[h] * cn[h] + upd
        m_s[h] = jnp.broadcast_to(m_new[h], m_s.shape[1:])

    if emit_state:
        cout_ref[...] = c_s[...]
        mout_ref[...] = m_s[...]


def _mlstm(p_main, p_gates, gate_b2, init, emit_state):
    b, t, _ = p_main.shape
    L = min(MLSTM_CHUNK, t)
    nc = t // L

    def cidx(d, j):
        return j + d * (nc - 1 - 2 * j)

    def col(cb):
        return pl.BlockSpec((None, L, M_WIDTH), lambda i, d, j: (i, cidx(d, j), cb))

    in_specs = [col(P_Q // M_WIDTH), col(P_K // M_WIDTH), col(P_V // M_WIDTH),
                pl.BlockSpec((None, L, LANES), lambda i, d, j: (i, cidx(d, j), d)),
                pl.BlockSpec((None, 1, LANES), lambda i, d, j: (d, 0, 0))]
    args = [p_main, p_main, p_main, p_gates, gate_b2]
    st_c = pl.BlockSpec((None, None, M_HEADS, M_DK, NV), lambda i, d, j: (i, d, 0, 0, 0))
    st_m = pl.BlockSpec((None, None, M_HEADS, 8, LANES), lambda i, d, j: (i, d, 0, 0, 0))
    if init is not None:
        in_specs += [st_c, st_m]
        args += list(init)
    out_specs = [pl.BlockSpec((None, None, L, M_WIDTH), lambda i, d, j: (d, i, cidx(d, j), 0))]
    out_shape = [jax.ShapeDtypeStruct((2, b, t, M_WIDTH), F32)]
    if emit_state:
        out_specs += [st_c, st_m]
        out_shape += [jax.ShapeDtypeStruct((b, 2, M_HEADS, M_DK, NV), F32),
                      jax.ShapeDtypeStruct((b, 2, M_HEADS, 8, LANES), F32)]
    return pl.pallas_call(
        functools.partial(_mlstm_kernel, has_init=init is not None, emit_state=emit_state),
        grid=(b, 2, nc),
        in_specs=in_specs,
        out_specs=out_specs,
        out_shape=out_shape,
        scratch_shapes=[pltpu.VMEM((M_HEADS, M_DK, NV), F32),
                        pltpu.VMEM((M_HEADS, 8, LANES), F32)],
        compiler_params=_cparams(("parallel", "parallel", "arbitrary")),
        name="mlstm",
    )(*args)


def _mlaprep_kernel(cq_ref, ckv_ref, kr1_ref, kr2_ref, cos_ref, sin_ref, cost_ref, sint_ref,
                    qg_ref, kvg_ref, wq1_ref, wq2_ref, wk_ref, wv_ref, qt_ref, k_ref, vt_ref):
    cos = cos_ref[...]
    sin = sin_ref[...]
    cq = cq_ref[...].astype(F32)
    cqn = (_rms(cq) * qg_ref[...]).astype(BF16)
    ckv = ckv_ref[...].astype(F32)
    ckvn = (_rms(ckv) * kvg_ref[...]).astype(BF16)
    a_scale = (B_NOPE + B_ROPE) ** -0.5 * LOG2E
    kr = kr1_ref[...].astype(F32) * cos + kr2_ref[...].astype(F32) * sin
    cos_t = cost_ref[...] * a_scale
    sin_t = sint_ref[...] * a_scale
    ones = jnp.ones((V_EXT - B_V, vt_ref.shape[1]), vt_ref.dtype)
    for c0 in range(0, B_WIDTH, LANES):
        v2 = _dot_nt(wv_ref[c0:c0 + LANES, :], ckvn).astype(vt_ref.dtype)
        for i in range(LANES // B_V):
            r0 = (c0 // B_V + i) * V_EXT
            vt_ref[r0:r0 + B_V, :] = v2[i * B_V:(i + 1) * B_V]
            vt_ref[r0 + B_V:r0 + V_EXT, :] = ones
    for h in range(B_HEADS):
        sl = slice(h * LANES, (h + 1) * LANES)
        q1 = _dot_nt(wq1_ref[sl, :], cqn)
        q2 = _dot_nt(wq2_ref[sl, :], cqn)
        qt_ref[sl, :] = (q1 * cos_t + q2 * sin_t).astype(qt_ref.dtype)
        k_ref[:, sl] = (_dot(ckvn, wk_ref[:, sl]) + kr).astype(k_ref.dtype)


def _mlaprep(p_main, tables, qg, kvg, wq1, wq2, wk, wv, tm):
    cos, sin, cos_t, sin_t = tables
    b, t, _ = p_main.shape
    tm = min(tm, t)
    hw = B_HEADS * LANES

    def tok(width, cb):
        return pl.BlockSpec((None, tm, width), lambda i, j: (i, j, cb))

    def full(a):
        return pl.BlockSpec(a.shape, lambda i, j: (0,) * a.ndim)

    return pl.pallas_call(
        _mlaprep_kernel,
        grid=(b, t // tm),
        in_specs=[tok(B_Q_RANK, P_CQ // B_Q_RANK), tok(LANES, P_CKV // LANES),
                  tok(LANES, P_KR1 // LANES), tok(LANES, P_KR2 // LANES),
                  pl.BlockSpec((tm, LANES), lambda i, j: (j, 0)),
                  pl.BlockSpec((tm, LANES), lambda i, j: (j, 0)),
                  pl.BlockSpec((LANES, tm), lambda i, j: (0, j)),
                  pl.BlockSpec((LANES, tm), lambda i, j: (0, j)),
                  full(qg), full(kvg), full(wq1), full(wq2), full(wk), full(wv)],
        out_specs=[pl.BlockSpec((None, hw, tm), lambda i, j: (i, 0, j)),
                   pl.BlockSpec((None, tm, hw), lambda i, j: (i, j, 0)),
                   pl.BlockSpec((None, B_HEADS * V_EXT, tm), lambda i, j: (i, 0, j))],
        out_shape=[jax.ShapeDtypeStruct((b, hw, t), BF16),
                   jax.ShapeDtypeStruct((b, t, hw), BF16),
                   jax.ShapeDtypeStruct((b, B_HEADS * V_EXT, t), BF16)],
        compiler_params=_cparams(("parallel", "parallel")),
        name="mlaprep",
    )(p_main, p_main, p_main, p_main, cos, sin, cos_t, sin_t, qg, kvg, wq1, wq2, wk, wv)


def _flash_kernel(q_ref, k_ref, v_ref, o_ref, s_s, p_s, *, tk, nk):
    tq = q_ref.shape[1]
    sls = [slice(h * LANES, (h + 1) * LANES) for h in range(2)]
    vsl = [slice(h * V_EXT, (h + 1) * V_EXT) for h in range(2)]

    def stage_a(h, slot, kb):
        s = _dot(kb, q_ref[sls[h], :])
        s_s[h, slot] = s
        return jnp.max(s, axis=0, keepdims=True)

    def stage_b(h, slot, mx, m):
        m_new = jnp.maximum(m, mx)
        p_s[h, slot] = jnp.exp2(s_s[h, slot] - m_new).astype(BF16)
        return m_new, jnp.exp2(m - m_new)

    def stage_c(h, slot, vb, alpha, acc):
        return alpha * acc + _dot(vb, p_s[h, slot])

    def kblock(j, h):
        return k_ref[pl.ds(pl.multiple_of(j * tk, tk), tk), sls[h]]

    def vblock(j, h):
        return v_ref[vsl[h], pl.ds(pl.multiple_of(j * tk, LANES), tk)]

    def trip(j, slot, state):
        new = []
        for h in range(2):
            mx, m, alpha, acc = state[h]
            mx_new = stage_a(h, slot, kblock(j, h))
            m, alpha_new = stage_b(h, 1 - slot, mx, m)
            acc = stage_c(h, slot, vblock(j - 2, h), alpha, acc)
            new.append((mx_new, m, alpha_new, acc))
        return tuple(new)

    zrow = jnp.zeros((1, tq), F32)
    neg = jnp.full((1, tq), NEG, F32)
    zacc = jnp.zeros((V_EXT, tq), F32)
    state = []
    for h in range(2):
        mx = stage_a(h, 0, k_ref[0:tk, sls[h]])
        if nk > 1:
            mx1 = stage_a(h, 1, k_ref[tk:2 * tk, sls[h]])
            m, alpha = stage_b(h, 0, mx, neg)
            state.append((mx1, m, alpha, zacc))
        else:
            state.append((mx, neg, zrow, zacc))
    state = tuple(state)
    if nk > 2:
        assert nk % 2 == 0

        def body(i, state):
            return trip(2 * i + 1, 1, trip(2 * i, 0, state))

        state = lax.fori_loop(1, nk // 2, body, state)
    outs = []
    for h in range(2):
        mx, m, alpha, acc = state[h]
        last = (nk - 1) % 2
        if nk > 1:
            acc = stage_c(h, 1 - last, v_ref[vsl[h], (nk - 2) * tk:(nk - 1) * tk], alpha, acc)
        m, alpha = stage_b(h, last, mx, m)
        acc = stage_c(h, last, v_ref[vsl[h], (nk - 1) * tk:nk * tk], alpha, acc)
        outs.append(acc[0:B_V] / acc[B_V:B_V + 1])
    o_ref[...] = jnp.concatenate(outs, axis=0).T.astype(o_ref.dtype)


def _key_chunk(tkeys):
    for c in (512, 384, 256, 128):
        if tkeys % c == 0 and (tkeys // c) % 2 == 0:
            return c
    raise ValueError(f"no key chunk for {tkeys} keys")


def _flash(q_t, k, v_t, tq, tk):
    b, _, t = q_t.shape
    tkeys = k.shape[1]
    tq = min(tq, t)
    tk = min(tk, tkeys)
    assert tkeys % tk == 0
    return pl.pallas_call(
        functools.partial(_flash_kernel, tk=tk, nk=tkeys // tk),
        grid=(b, B_HEADS // 2, t // tq),
        in_specs=[pl.BlockSpec((None, 2 * LANES, tq), lambda i, hp, j: (i, hp, j)),
                  pl.BlockSpec((None, tkeys, 2 * LANES), lambda i, hp, j: (i, 0, hp)),
                  pl.BlockSpec((None, 2 * V_EXT, tkeys), lambda i, hp, j: (i, hp, 0))],
        out_specs=pl.BlockSpec((None, tq, LANES), lambda i, hp, j: (i, j, hp)),
        out_shape=jax.ShapeDtypeStruct((b, t, B_WIDTH), BF16),
        scratch_shapes=[pltpu.VMEM((2, 2, tk, tq), F32), pltpu.VMEM((2, 2, tk, tq), BF16)],
        compiler_params=_cparams(("parallel", "parallel", "arbitrary")),
        name="flash",
    )(q_t, k, v_t)


NA_PAIR_ROWS = NA_KH + 2
NA_PAIR_KEYS = NA_PAIR_ROWS * GRID_W
NA_VARIANTS = 5


def _natten_pair_window(r, rows):
    return jnp.clip(r - NA_KH // 2, 0, rows - NA_PAIR_ROWS)


def _natten_kernel(q_ref, kt_ref, v_ref, kc_ref, vc_ref, bias_ref, o_ref, *, rb, rows):
    blk = pl.program_id(2)
    kc = kc_ref[...]
    vc = vc_ref[...]
    tq = rb * GRID_W
    pq = 2 * GRID_W
    lane = lax.broadcasted_iota(jnp.int32, (tq, LANES), 1)
    q_all = q_ref[...]
    npair = rb // 2
    qh, s_c, m_c = [], [], []
    for h in range(2):
        in_head = (lane >= h * C_HEAD_DIM) & (lane < (h + 1) * C_HEAD_DIM)
        qh.append(jnp.where(in_head, q_all, jnp.zeros_like(q_all)))
        s_c.append(_dot_nt(qh[h], kc))
        m_c.append(jnp.max(s_c[h], axis=-1, keepdims=True))

    def scores(h, i):
        r = blk * rb + 2 * i
        ws = _natten_pair_window(r, rows)
        koff = pl.multiple_of(ws * GRID_W, LANES)
        s_w = _dot(qh[h][i * pq:(i + 1) * pq], kt_ref[:, pl.ds(koff, NA_PAIR_KEYS)])
        return koff, s_w + bias_ref[h, (r - ws) // 2]

    def softmax(h, i, s_w):
        rsl = slice(i * pq, (i + 1) * pq)
        m = jnp.maximum(jnp.max(s_w, axis=-1, keepdims=True), m_c[h][rsl])
        p_w = jnp.exp2(s_w - m)
        p_ci = jnp.exp2(s_c[h][rsl] - m)
        l = jnp.sum(p_w, axis=-1, keepdims=True) + jnp.sum(p_ci, axis=-1, keepdims=True)
        return p_w.astype(BF16), p_ci.astype(BF16), l

    items = [(h, i) for h in range(2) for i in range(npair)]
    o_w = {}
    p_c = {}
    l_all = {}
    pending = scores(*items[0])
    for n, (h, i) in enumerate(items):
        koff, s_w = pending
        if n + 1 < len(items):
            pending = scores(*items[n + 1])
        p_w, p_c[h, i], l_all[h, i] = softmax(h, i, s_w)
        o_w[h, i] = _dot(p_w, v_ref[pl.ds(koff, NA_PAIR_KEYS), :])
    heads = []
    for h in range(2):
        o = (jnp.concatenate([o_w[h, i] for i in range(npair)], axis=0)
             + _dot(jnp.concatenate([p_c[h, i] for i in range(npair)], axis=0), vc))
        heads.append(o / jnp.concatenate([l_all[h, i] for i in range(npair)], axis=0))
    o_ref[...] = jnp.where(lane < C_HEAD_DIM, heads[0], heads[1]).astype(o_ref.dtype)


def _natten(q, k_t, v, qkv_c, bias_t, rb):
    b, t, _ = q.shape
    tc = qkv_c.shape[1]
    rows = t // GRID_W
    rb = min(rb, rows)
    assert rows % 2 == 0 and rows >= NA_PAIR_ROWS + 4 and rb % 2 == 0
    nhp = C_HEADS // 2
    return pl.pallas_call(
        functools.partial(_natten_kernel, rb=rb, rows=rows),
        grid=(b, nhp, rows // rb),
        in_specs=[pl.BlockSpec((None, rb * GRID_W, LANES), lambda i, hp, j: (i, j, hp)),
                  pl.BlockSpec((None, LANES, t), lambda i, hp, j: (i, hp, 0)),
                  pl.BlockSpec((None, t, LANES), lambda i, hp, j: (i, 0, hp)),
                  pl.BlockSpec((None, tc, LANES), lambda i, hp, j: (i, 0, nhp + hp)),
                  pl.BlockSpec((None, tc, LANES), lambda i, hp, j: (i, 0, 2 * nhp + hp)),
                  pl.BlockSpec((None,) + bias_t.shape[1:], lambda i, hp, j: (hp, 0, 0, 0, 0))],
        out_specs=pl.BlockSpec((None, rb * GRID_W, LANES), lambda i, hp, j: (i, j, hp)),
        out_shape=jax.ShapeDtypeStruct((b, t, C_HEADS * C_HEAD_DIM), BF16),
        compiler_params=_cparams(("parallel", "parallel", "arbitrary")),
        name="natten",
    )(q, k_t, v, qkv_c, qkv_c, bias_t)


def _natten_bias_table(rel_bias):
    qc = np.arange(GRID_W)[:, None]
    kcol = np.arange(GRID_W)[None, :]
    cs = np.clip(qc - NA_KW // 2, 0, GRID_W - NA_KW)
    valid = (kcol >= cs) & (kcol < cs + NA_KW)
    pad = GRID_W - NA_KW
    rbp = jnp.pad(rel_bias.astype(F32) * LOG2E, ((0, 0), (0, 0), (pad, pad)))
    toe = jnp.stack([rbp[:, :, GRID_W - 1 - q:2 * GRID_W - 1 - q] for q in range(GRID_W)], axis=2)
    toe = jnp.where(valid[None, None], toe, NEG)
    masked = jnp.full_like(toe[:, 0], NEG)
    own = ((0, 0), (0, 0), (0, 1), (2, 2), (2, 2))
    variants = []
    for v in range(NA_VARIANTS):
        halves = []
        for j in range(2):
            a = own[v][j]
            cols = [toe[:, x - 2 * v - j + NA_KH - 1] if a <= x < a + NA_KH else masked
                    for x in range(NA_PAIR_ROWS)]
            halves.append(jnp.concatenate(cols, axis=-1))
        variants.append(jnp.concatenate(halves, axis=1))
    tab = jnp.stack(variants, axis=1)
    return tab.reshape(C_HEADS // 2, 2, NA_VARIANTS, 2 * GRID_W, NA_PAIR_KEYS)


def _mlp_tail(x1, g2, sh2, sc2, n2g, w1_ref, w2_ref):
    h = (_rms(x1) * n2g * (1.0 + sc2) + sh2).astype(BF16)
    acc = jnp.zeros_like(x1)
    chunk = 1024
    for c0 in range(0, D_FF, chunk):
        a = jnp.maximum(_dot(h, w1_ref[:, c0:c0 + chunk]), 0.0)
        acc = acc + _dot((a * a).astype(BF16), w2_ref[c0:c0 + chunk, :])
    return x1 + g2 * acc


def _post_ab_kernel(x_ref, hf_ref, hb_ref, og_ref, bl_ref, mg_ref, g1_ref, sh2_ref, sc2_ref,
                    g2_ref, n2g_ref, wo_ref, w1_ref, w2_ref, o_ref):
    hm = hf_ref[...] + hb_ref[...]
    og = jax.nn.sigmoid(og_ref[...].astype(F32))
    mg = mg_ref[...]
    y = jnp.zeros(x_ref.shape, F32)
    for h in range(M_HEADS):
        sl = slice(h * M_DV, (h + 1) * M_DV)
        hn = (_rms(hm[:, sl]) * mg[:, sl] * og[:, sl]).astype(BF16)
        y = y + _dot(hn, wo_ref[sl, :])
    y = y + _dot(bl_ref[...], wo_ref[M_WIDTH:, :])
    x1 = x_ref[...] + g1_ref[...] * y
    o_ref[...] = _mlp_tail(x1, g2_ref[...], sh2_ref[...], sc2_ref[...], n2g_ref[...],
                           w1_ref, w2_ref)


def _post_c_kernel(x_ref, y_ref, g1_ref, sh2_ref, sc2_ref, g2_ref, n2g_ref, fg_ref,
                   wo_ref, w1_ref, w2_ref, o_ref):
    x1 = x_ref[...] + g1_ref[...] * _dot(y_ref[...], wo_ref[...])
    x2 = _mlp_tail(x1, g2_ref[...], sh2_ref[...], sc2_ref[...], n2g_ref[...], w1_ref, w2_ref)
    o_ref[...] = _rms(x2) * fg_ref[...]


def _resident(a):
    return pl.BlockSpec(a.shape, lambda i, j: (0,) * a.ndim, pipeline_mode=pl.Buffered(1))


def _post_ab(x, h_dirs, p_main, b_lat, mg, mods, n2g, wo, w1, w2, tm):
    b, t, d = x.shape
    tm = min(tm, t)
    g1, sh2, sc2, g2 = mods
    bm = g1.shape[0]
    mod_map = (lambda i, j: (i, 0, 0)) if bm == b else (lambda i, j: (0, 0, 0))
    mod_spec = pl.BlockSpec((None, 1, d), mod_map)
    tok = lambda w, cb: pl.BlockSpec((None, tm, w), lambda i, j: (i, j, cb))
    return pl.pallas_call(
        _post_ab_kernel,
        grid=(b, t // tm),
        in_specs=[tok(d, 0),
                  pl.BlockSpec((None, None, tm, M_WIDTH), lambda i, j: (0, i, j, 0)),
                  pl.BlockSpec((None, None, tm, M_WIDTH), lambda i, j: (1, i, j, 0)),
                  tok(M_WIDTH, P_O // M_WIDTH), tok(B_WIDTH, 0),
                  pl.BlockSpec((1, M_WIDTH), lambda i, j: (0, 0)),
                  mod_spec, mod_spec, mod_spec, mod_spec,
                  pl.BlockSpec((1, d), lambda i, j: (0, 0)),
                  _resident(wo), _resident(w1), _resident(w2)],
        out_specs=tok(d, 0),
        out_shape=jax.ShapeDtypeStruct((b, t, d), F32),
        compiler_params=_cparams(("parallel", "parallel")),
        name="post_ab",
    )(x, h_dirs, h_dirs, p_main, b_lat, mg, g1, sh2, sc2, g2, n2g.reshape(1, d), wo, w1, w2)


def _post_c(x, y, mods, n2g, fg, wo, w1, w2, tm):
    b, t, d = x.shape
    tm = min(tm, t)
    g1, sh2, sc2, g2 = mods
    mod_spec = pl.BlockSpec((None, 1, d), lambda i, j: (i, 0, 0))
    tok = pl.BlockSpec((None, tm, d), lambda i, j: (i, j, 0))
    vec = pl.BlockSpec((1, d), lambda i, j: (0, 0))
    return pl.pallas_call(
        _post_c_kernel,
        grid=(b, t // tm),
        in_specs=[tok, tok, mod_spec, mod_spec, mod_spec, mod_spec, vec, vec,
                  _resident(wo), _resident(w1), _resident(w2)],
        out_specs=tok,
        out_shape=jax.ShapeDtypeStruct((b, t, d), F32),
        compiler_params=_cparams(("parallel", "parallel")),
        name="post_c",
    )(x, y, g1, sh2, sc2, g2, n2g.reshape(1, d), fg.reshape(1, d), wo, w1, w2)


def _rope_tables(t):
    pos = np.arange(t)
    row = (pos // GRID_W).astype(np.float32)
    colp = (pos % GRID_W).astype(np.float32)
    n_f = B_ROPE // 4
    freqs = (ROPE_BASE ** (-np.arange(n_f, dtype=np.float32) / n_f)).astype(np.float32)
    ang = np.concatenate([row[:, None] * freqs, colp[:, None] * freqs], axis=-1)
    cos = np.zeros((t, LANES), np.float32)
    sin = np.zeros((t, LANES), np.float32)
    cos[:, :B_NOPE] = 1.0
    cos[:, B_NOPE:B_NOPE + 16] = np.cos(ang)
    cos[:, B_NOPE + 16:B_NOPE + 32] = np.cos(ang)
    sin[:, B_NOPE:B_NOPE + 16] = np.sin(ang)
    sin[:, B_NOPE + 16:B_NOPE + 32] = np.sin(ang)
    return tuple(jnp.asarray(a) for a in (cos, sin, cos.T.copy(), sin.T.copy()))


def _identity_tables(t):
    cos = np.zeros((t, LANES), np.float32)
    cos[:, :B_NOPE + B_ROPE] = 1.0
    sin = np.zeros((t, LANES), np.float32)
    return tuple(jnp.asarray(a) for a in (cos, sin, cos.T.copy(), sin.T.copy()))


def _layer0_weights(w_in, gate_b, w_uq, w_ukv):
    d = w_in.shape[0]
    half = B_ROPE // 2
    o_g = 4 * M_WIDTH
    o_cq = o_g + 4 * M_HEADS
    o_ckv = o_cq + B_Q_RANK
    o_kr = o_ckv + B_KV_RANK
    kr = w_in[:, o_kr:o_kr + B_ROPE]
    z = lambda n: jnp.zeros((d, n), w_in.dtype)
    kr1 = jnp.concatenate([z(B_NOPE), kr, z(LANES - B_NOPE - B_ROPE)], axis=1)
    kr2 = jnp.concatenate([z(B_NOPE), -kr[:, half:], kr[:, :half], z(LANES - B_NOPE - B_ROPE)], axis=1)
    wg = w_in[:, o_g:o_g + 4 * M_HEADS]
    gpad = z(LANES - 2 * M_HEADS)
    w0 = jnp.concatenate([w_in[:, :o_g], w_in[:, o_cq:o_kr], kr1, kr2,
                          wg[:, :2 * M_HEADS], gpad, wg[:, 2 * M_HEADS:], gpad], axis=1).astype(BF16)
    gb = jnp.zeros((2, 1, LANES), F32)
    gb = gb.at[0, 0, :2 * M_HEADS].set(gate_b[:2 * M_HEADS]).at[1, 0, :2 * M_HEADS].set(gate_b[2 * M_HEADS:])

    qr = B_Q_RANK
    wq = w_uq.reshape(qr, B_HEADS, B_NOPE + B_ROPE)
    nope, ra, rb = wq[..., :B_NOPE], wq[..., B_NOPE:B_NOPE + half], wq[..., B_NOPE + half:]
    zq = jnp.zeros((qr, B_HEADS, LANES - B_NOPE - B_ROPE), w_uq.dtype)
    wq1 = jnp.concatenate([nope, ra, rb, zq], axis=-1).reshape(qr, B_HEADS * LANES).astype(BF16)
    wq2 = jnp.concatenate([jnp.zeros_like(nope), -rb, ra, zq], axis=-1).reshape(qr, B_HEADS * LANES).astype(BF16)
    wkv = w_ukv.reshape(B_KV_RANK, B_HEADS, B_NOPE + B_V)
    wk = jnp.concatenate([wkv[..., :B_NOPE], jnp.zeros((B_KV_RANK, B_HEADS, LANES - B_NOPE), w_ukv.dtype)],
                         axis=-1).reshape(B_KV_RANK, B_HEADS * LANES).astype(BF16)
    wv = wkv[..., B_NOPE:].reshape(B_KV_RANK, B_WIDTH).astype(BF16)
    return w0, gb, wq1.T, wq2.T, wk, wv.T


def _split_mod(mod_rows):
    return [m[:, None, :] for m in jnp.split(mod_rows, 6, axis=-1)]


def kernel(x, c, ctx, c_ctx, ada_w, ada_b, norm1_g, norm2_g, mlp_w1, mlp_w2, ab_w_in, ab_gate_b,
           ab_m_norm_g, ab_q_norm_g, ab_kv_norm_g, ab_w_uq, ab_w_ukv, ab_w_out, na_w_in,
           na_rel_bias, na_w_out, final_norm_g):
    b, t, d = x.shape
    tc = ctx.shape[1]
    c_rows = jnp.concatenate([c, c_ctx[None, :], jnp.zeros((8 - b - 1, d), F32)], axis=0)
    mod = _ada(c_rows, ada_w, ada_b)

    sh1, sc1, g1, sh2, sc2, g2 = _split_mod(mod[0, :b])
    csh1, csc1, cg1, csh2, csc2, cg2 = _split_mod(mod[0, b:b + 1])
    w0, gb, wq1, wq2, wk, wv = _layer0_weights(ab_w_in[0], ab_gate_b[0], ab_w_uq[0], ab_w_ukv[0])
    widths, dtypes = (P_MAIN, P_GATES), (BF16, F32)
    p_main, p_gates = _modproj(x, sh1, sc1, norm1_g[0], w0, widths, dtypes, tm=512)
    pc_main, pc_gates = _modproj(ctx, csh1, csc1, norm1_g[0], w0, widths, dtypes, tm=256)

    hc_dirs, st_c, st_m = _mlstm(pc_main, pc_gates, gb, None, True)
    (h_dirs,) = _mlstm(p_main, p_gates, gb, (st_c, st_m), False)

    qg = ab_q_norm_g[0].reshape(1, -1)
    kvg = ab_kv_norm_g[0].reshape(1, -1)
    q_l, k_l, v_l = _mlaprep(p_main, _rope_tables(t), qg, kvg, wq1, wq2, wk, wv, tm=512)
    q_c, k_c, v_c = _mlaprep(pc_main, _identity_tables(tc), qg, kvg, wq1, wq2, wk, wv, tm=256)
    k_all = jnp.concatenate([k_l, k_c], axis=1)
    v_all = jnp.concatenate([v_l, v_c], axis=2)
    b_lat = _flash(q_l, k_all, v_all, tq=2048, tk=_key_chunk(t + tc))
    b_ctx = _flash(q_c, k_c, v_c, tq=256, tk=tc)

    wo = ab_w_out[0].astype(BF16)
    w1 = mlp_w1[0].astype(BF16)
    w2 = mlp_w2[0].astype(BF16)
    mg = ab_m_norm_g[0].reshape(1, M_WIDTH)
    x = _post_ab(x, h_dirs, p_main, b_lat, mg, (g1, sh2, sc2, g2), norm2_g[0], wo, w1, w2, tm=512)
    ctx = _post_ab(ctx, hc_dirs, pc_main, b_ctx, mg, (cg1, csh2, csc2, cg2), norm2_g[0],
                   wo, w1, w2, tm=256)

    sh1, sc1, g1, sh2, sc2, g2 = _split_mod(mod[1, :b])
    csh1, csc1 = _split_mod(mod[1, b:b + 1])[:2]
    wn = na_w_in[0].astype(BF16)
    cw = C_HEADS * C_HEAD_DIM
    w_q = (na_w_in[0][:, :cw] * (C_HEAD_DIM ** -0.5 * LOG2E)).astype(BF16)
    w_qv = jnp.concatenate([w_q, wn[:, 2 * cw:]], axis=1)
    q_n, v_n, k_t = _modproj(x, sh1, sc1, norm1_g[1], w_qv, (cw, cw), (BF16, BF16), tm=512,
                             w_t=wn[:, cw:2 * cw].T)
    (qkv_c,) = _modproj(ctx, csh1, csc1, norm1_g[1], wn, (3 * cw,), (BF16,), tm=256)
    bias_t = _natten_bias_table(na_rel_bias[0])
    y = _natten(q_n, k_t, v_n, qkv_c, bias_t, rb=8)
    return _post_c(x, y, (g1, sh2, sc2, g2), norm2_g[1], final_norm_g,
                   na_w_out[0].astype(BF16), mlp_w1[1].astype(BF16), mlp_w2[1].astype(BF16), tm=512)
```

```python
import functools

import numpy as np
import jax
import jax.numpy as jnp
from jax import lax
from jax.experimental import pallas as pl
from jax.experimental.pallas import tpu as pltpu

F32 = jnp.float32
BF16 = jnp.bfloat16

D_MODEL = 1024
DEPTH = 2
GRID_W = 64
EPS = 1e-6
M_HEADS = 4
M_DK = 128
M_DV = 128
B_HEADS = 8
B_Q_RANK = 256
B_KV_RANK = 128
B_NOPE = 64
B_ROPE = 32
B_V = 64
ROPE_BASE = 10000.0
C_HEADS = 16
C_HEAD_DIM = 64
NA_KH = 8
NA_KW = 16
D_FF = 4 * D_MODEL
M_WIDTH = M_HEADS * M_DV
B_WIDTH = B_HEADS * B_V

LANES = 128
NEG = -1e30
LOG2E = float(np.log2(np.e))
VMEM_LIMIT = 56 * 1024 * 1024

P_K, P_CQ, P_CKV, P_KR1, P_KR2 = 0, 512, 768, 896, 1024
P_MAIN = 1152
T_Q, T_V, T_O = 0, 512, 1024
P_GATES = 256
MLSTM_CHUNK = 256
NV = 2 * M_DV
BF16_ROWS = 16
V_EXT = B_V + BF16_ROWS


def _cparams(sem):
    return pltpu.CompilerParams(dimension_semantics=sem, vmem_limit_bytes=VMEM_LIMIT)


def _rms(x):
    return x * lax.rsqrt(jnp.mean(x * x, axis=-1, keepdims=True) + EPS)


def _dot(a, b):
    return jnp.dot(a, b, preferred_element_type=F32)


def _dot_nt(a, b):
    return lax.dot_general(a, b, (((1,), (1,)), ((), ())), preferred_element_type=F32)


def _ada_kernel(c_ref, w_ref, b_ref, o_ref):
    c = c_ref[...]
    s = c * jax.nn.sigmoid(c)
    o_ref[...] = jnp.dot(s, w_ref[...], precision=lax.Precision.HIGHEST,
                         preferred_element_type=F32) + b_ref[...]


def _ada(c_rows, ada_w, ada_b):
    depth, d, n = ada_w.shape
    tn = 1536
    return pl.pallas_call(
        _ada_kernel,
        grid=(depth, n // tn),
        in_specs=[pl.BlockSpec((8, d), lambda l, j: (0, 0)),
                  pl.BlockSpec((None, d, tn), lambda l, j: (l, 0, j)),
                  pl.BlockSpec((None, 1, tn), lambda l, j: (l, 0, j))],
        out_specs=pl.BlockSpec((None, 8, tn), lambda l, j: (l, 0, j)),
        out_shape=jax.ShapeDtypeStruct((depth, 8, n), F32),
        compiler_params=_cparams(("arbitrary", "arbitrary")),
        name="ada",
    )(c_rows, ada_w, ada_b.reshape(depth, 1, n))


def _modproj_kernel(x_ref, sh_ref, sc_ref, g_ref, w_ref, *refs, widths, has_t, chunk):
    wt_ref = refs[0] if has_t else None
    o_refs = refs[1:] if has_t else refs
    x = x_ref[...]
    h = _rms(x) * g_ref[...] * (1.0 + sc_ref[...]) + sh_ref[...]
    hb = h.astype(BF16)
    off = 0
    for o_ref, n in zip(o_refs, widths):
        for c0 in range(0, n, chunk):
            c1 = min(n, c0 + chunk)
            o_ref[:, c0:c1] = _dot(hb, w_ref[:, off + c0:off + c1]).astype(o_ref.dtype)
        off += n
    if has_t:
        ot_ref = o_refs[-1]
        for c0 in range(0, wt_ref.shape[0], chunk):
            ot_ref[c0:c0 + chunk, :] = _dot_nt(wt_ref[c0:c0 + chunk, :], hb).astype(ot_ref.dtype)


def _modproj(x, shift, scale, g, w, widths, dtypes, tm, w_t=None):
    b, t, d = x.shape
    bm = shift.shape[0]
    mod_map = (lambda i, j: (i, 0, 0)) if bm == b else (lambda i, j: (0, 0, 0))
    tm = min(tm, t)
    in_specs = [pl.BlockSpec((None, tm, d), lambda i, j: (i, j, 0)),
                pl.BlockSpec((None, 1, d), mod_map),
                pl.BlockSpec((None, 1, d), mod_map),
                pl.BlockSpec((1, d), lambda i, j: (0, 0)),
                pl.BlockSpec(w.shape, lambda i, j: (0, 0))]
    args = [x, shift, scale, g.reshape(1, d), w]
    out_specs = [pl.BlockSpec((None, tm, n), lambda i, j: (i, j, 0)) for n in widths]
    out_shape = [jax.ShapeDtypeStruct((b, t, n), dt) for n, dt in zip(widths, dtypes)]
    if w_t is not None:
        in_specs.append(pl.BlockSpec(w_t.shape, lambda i, j: (0, 0)))
        args.append(w_t)
        out_specs.append(pl.BlockSpec((None, w_t.shape[0], tm), lambda i, j: (i, 0, j)))
        out_shape.append(jax.ShapeDtypeStruct((b, w_t.shape[0], t), BF16))
    return pl.pallas_call(
        functools.partial(_modproj_kernel, widths=widths, has_t=w_t is not None, chunk=512),
        grid=(b, t // tm),
        in_specs=in_specs,
        out_specs=out_specs,
        out_shape=out_shape,
        compiler_params=_cparams(("parallel", "parallel")),
        name="modproj",
    )(*args)


def _mlstm_kernel(*refs, has_init, emit_state):
    qt_ref, k_ref, vt_ref, g_ref, gb_ref = refs[:5]
    pos = 5
    if has_init:
        c0_ref, m0_ref = refs[pos:pos + 2]
        pos += 2
    h_ref = refs[pos]
    pos += 1
    if emit_state:
        cout_ref, mout_ref = refs[pos:pos + 2]
        pos += 2
    c_s, m_s = refs[pos:pos + 2]

    d = pl.program_id(1)
    j = pl.program_id(2)
    L = k_ref.shape[0]

    @pl.when(j == 0)
    def _():
        if has_init:
            c_s[...] = c0_ref[...]
            m_s[...] = m0_ref[...]
        else:
            c_s[...] = jnp.zeros_like(c_s)
            m_s[...] = jnp.zeros_like(m_s)

    r = lax.broadcasted_iota(jnp.int32, (L, L), 0)
    c = lax.broadcasted_iota(jnp.int32, (L, L), 1)
    sgn = 1 - 2 * d
    keep = (r - c) * sgn >= 0
    keep_t = (c - r) * sgn >= 0

    gates = g_ref[...] + gb_ref[...]
    ls = jax.nn.log_sigmoid(gates)
    gates_t = gates.T[0:8]
    ls_t = jax.nn.log_sigmoid(gates_t)

    def split3(a):
        hi = a.astype(BF16)
        mid = (a - hi.astype(F32)).astype(BF16)
        lo = (a - hi.astype(F32) - mid.astype(F32)).astype(BF16)
        return hi, mid, lo

    trib = keep.astype(BF16)
    bc3 = _dot(trib, jnp.concatenate(split3(ls), axis=1))
    b_cols = bc3[:, 0:LANES] + bc3[:, LANES:2 * LANES] + bc3[:, 2 * LANES:]
    br3 = _dot_nt(jnp.concatenate(split3(ls_t), axis=0), trib)
    b_rows = br3[0:8] + br3[8:16] + br3[16:24]
    b_tot = jnp.sum(ls, axis=0, keepdims=True)

    ones_rows = jnp.ones((M_DV, L), BF16)
    scale = M_DK ** -0.5
    hsl = [slice(h * M_DK, (h + 1) * M_DK) for h in range(M_HEADS)]

    cn, m, s_raw, qc = [], [], [], []
    for h in range(M_HEADS):
        cn.append(c_s[h])
        m.append(m_s[h][0:1, 0:1])
        lhs = jnp.concatenate([k_ref[:, hsl[h]], cn[h].astype(BF16)], axis=0)
        sq = _dot(lhs, qt_ref[hsl[h], :])
        s_raw.append(sq[0:L])
        qc.append(sq[L:])

    s_w, w_inter, m_t, vext, vw, decay, m_new = [], [], [], [], [], [], []
    for h in range(M_HEADS):
        col = gates[:, h:h + 1] - b_cols[:, 4 + h:5 + h]
        li_r = gates_t[h:h + 1, :]
        b_r = b_rows[4 + h:5 + h, :]
        b_last = b_tot[:, 4 + h:5 + h]
        dm = jnp.where(keep_t, b_r + col, NEG)
        inter = b_r + m[h]
        m_t.append(jnp.maximum(jnp.max(dm, axis=0, keepdims=True), inter))
        s_w.append((s_raw[h] * (scale * jnp.exp(dm - m_t[h]))).astype(BF16))
        w_inter.append(jnp.exp(inter - m_t[h]) * scale)
        g_r = b_last - b_r + li_r
        m_new.append(jnp.maximum(b_last + m[h], jnp.max(g_r, axis=1, keepdims=True)))
        decay.append(jnp.exp(b_last + m[h] - m_new[h]))
        vext.append(jnp.concatenate([vt_ref[hsl[h], :], ones_rows], axis=0))
        vw.append((vext[h].astype(F32) * jnp.exp(g_r - m_new[h])).astype(BF16))

    for h in range(M_HEADS):
        nd = w_inter[h] * qc[h] + _dot(vext[h], s_w[h])
        den = nd[M_DV:M_DV + 1]
        h_ref[hsl[h], :] = nd[:M_DV] / jnp.maximum(jnp.abs(den), jnp.exp(-m_t[h]))
        c_s[h] = decay[h] * cn[h] + _dot(vw[h], k_ref[:, hsl[h]])
        m_s[h] = jnp.broadcast_to(m_new[h], m_s.shape[1:])

    if emit_state:
        cout_ref[...] = c_s[...]
        mout_ref[...] = m_s[...]


def _mlstm(p_main, p_gates, qvo_t, gate_b2, init, emit_state):
    b, t, _ = p_main.shape
    L = min(MLSTM_CHUNK, t)
    nc = t // L

    def cidx(d, j):
        return j + d * (nc - 1 - 2 * j)

    def rows(rb):
        return pl.BlockSpec((None, M_WIDTH, L), lambda i, d, j: (i, rb, cidx(d, j)))

    in_specs = [rows(T_Q // M_WIDTH),
                pl.BlockSpec((None, L, M_WIDTH), lambda i, d, j: (i, cidx(d, j), P_K // M_WIDTH)),
                rows(T_V // M_WIDTH),
                pl.BlockSpec((None, L, LANES), lambda i, d, j: (i, cidx(d, j), d)),
                pl.BlockSpec((None, 1, LANES), lambda i, d, j: (d, 0, 0))]
    args = [qvo_t, p_main, qvo_t, p_gates, gate_b2]
    st_c = pl.BlockSpec((None, None, M_HEADS, NV, M_DK), lambda i, d, j: (i, d, 0, 0, 0))
    st_m = pl.BlockSpec((None, None, M_HEADS, 8, LANES), lambda i, d, j: (i, d, 0, 0, 0))
    if init is not None:
        in_specs += [st_c, st_m]
        args += list(init)
    out_specs = [pl.BlockSpec((None, None, M_WIDTH, L), lambda i, d, j: (d, i, 0, cidx(d, j)))]
    out_shape = [jax.ShapeDtypeStruct((2, b, M_WIDTH, t), F32)]
    if emit_state:
        out_specs += [st_c, st_m]
        out_shape += [jax.ShapeDtypeStruct((b, 2, M_HEADS, NV, M_DK), F32),
                      jax.ShapeDtypeStruct((b, 2, M_HEADS, 8, LANES), F32)]
    return pl.pallas_call(
        functools.partial(_mlstm_kernel, has_init=init is not None, emit_state=emit_state),
        grid=(b, 2, nc),
        in_specs=in_specs,
        out_specs=out_specs,
        out_shape=out_shape,
        scratch_shapes=[pltpu.VMEM((M_HEADS, NV, M_DK), F32),
                        pltpu.VMEM((M_HEADS, 8, LANES), F32)],
        compiler_params=_cparams(("parallel", "parallel", "arbitrary")),
        name="mlstm",
    )(*args)


def _mlaprep_kernel(cq_ref, ckv_ref, kr1_ref, kr2_ref, cos_ref, sin_ref, cost_ref, sint_ref,
                    qg_ref, kvg_ref, wq1_ref, wq2_ref, wk_ref, wv_ref, qt_ref, k_ref, vt_ref):
    cos = cos_ref[...]
    sin = sin_ref[...]
    cq = cq_ref[...].astype(F32)
    cqn = (_rms(cq) * qg_ref[...]).astype(BF16)
    ckv = ckv_ref[...].astype(F32)
    ckvn = (_rms(ckv) * kvg_ref[...]).astype(BF16)
    a_scale = (B_NOPE + B_ROPE) ** -0.5 * LOG2E
    kr = kr1_ref[...].astype(F32) * cos + kr2_ref[...].astype(F32) * sin
    cos_t = cost_ref[...] * a_scale
    sin_t = sint_ref[...] * a_scale
    ones = jnp.ones((V_EXT - B_V, vt_ref.shape[1]), vt_ref.dtype)
    for c0 in range(0, B_WIDTH, LANES):
        v2 = _dot_nt(wv_ref[c0:c0 + LANES, :], ckvn).astype(vt_ref.dtype)
        for i in range(LANES // B_V):
            r0 = (c0 // B_V + i) * V_EXT
            vt_ref[r0:r0 + B_V, :] = v2[i * B_V:(i + 1) * B_V]
            vt_ref[r0 + B_V:r0 + V_EXT, :] = ones
    for h in range(B_HEADS):
        sl = slice(h * LANES, (h + 1) * LANES)
        q1 = _dot_nt(wq1_ref[sl, :], cqn)
        q2 = _dot_nt(wq2_ref[sl, :], cqn)
        qt_ref[sl, :] = (q1 * cos_t + q2 * sin_t).astype(qt_ref.dtype)
        k_ref[:, sl] = (_dot(ckvn, wk_ref[:, sl]) + kr).astype(k_ref.dtype)


def _mlaprep(p_main, tables, qg, kvg, wq1, wq2, wk, wv, tm):
    cos, sin, cos_t, sin_t = tables
    b, t, _ = p_main.shape
    tm = min(tm, t)
    hw = B_HEADS * LANES

    def tok(width, cb):
        return pl.BlockSpec((None, tm, width), lambda i, j: (i, j, cb))

    def full(a):
        return pl.BlockSpec(a.shape, lambda i, j: (0,) * a.ndim)

    return pl.pallas_call(
        _mlaprep_kernel,
        grid=(b, t // tm),
        in_specs=[tok(B_Q_RANK, P_CQ // B_Q_RANK), tok(LANES, P_CKV // LANES),
                  tok(LANES, P_KR1 // LANES), tok(LANES, P_KR2 // LANES),
                  pl.BlockSpec((tm, LANES), lambda i, j: (j, 0)),
                  pl.BlockSpec((tm, LANES), lambda i, j: (j, 0)),
                  pl.BlockSpec((LANES, tm), lambda i, j: (0, j)),
                  pl.BlockSpec((LANES, tm), lambda i, j: (0, j)),
                  full(qg), full(kvg), full(wq1), full(wq2), full(wk), full(wv)],
        out_specs=[pl.BlockSpec((None, hw, tm), lambda i, j: (i, 0, j)),
                   pl.BlockSpec((None, tm, hw), lambda i, j: (i, j, 0)),
                   pl.BlockSpec((None, B_HEADS * V_EXT, tm), lambda i, j: (i, 0, j))],
        out_shape=[jax.ShapeDtypeStruct((b, hw, t), BF16),
                   jax.ShapeDtypeStruct((b, t, hw), BF16),
                   jax.ShapeDtypeStruct((b, B_HEADS * V_EXT, t), BF16)],
        compiler_params=_cparams(("parallel", "parallel")),
        name="mlaprep",
    )(p_main, p_main, p_main, p_main, cos, sin, cos_t, sin_t, qg, kvg, wq1, wq2, wk, wv)


def _flash_kernel(q_ref, k_ref, v_ref, o_ref, s_s, p_s, *, tk, nk):
    tq = q_ref.shape[1]
    sls = [slice(h * LANES, (h + 1) * LANES) for h in range(2)]
    vsl = [slice(h * V_EXT, (h + 1) * V_EXT) for h in range(2)]

    def stage_a(h, slot, kb):
        s = _dot(kb, q_ref[sls[h], :])
        s_s[h, slot] = s
        return jnp.max(s, axis=0, keepdims=True)

    def stage_b(h, slot, mx, m):
        m_new = jnp.maximum(m, mx)
        p_s[h, slot] = jnp.exp2(s_s[h, slot] - m_new).astype(BF16)
        return m_new, jnp.exp2(m - m_new)

    def stage_c(h, slot, vb, alpha, acc):
        return alpha * acc + _dot(vb, p_s[h, slot])

    def kblock(j, h):
        return k_ref[pl.ds(pl.multiple_of(j * tk, tk), tk), sls[h]]

    def vblock(j, h):
        return v_ref[vsl[h], pl.ds(pl.multiple_of(j * tk, LANES), tk)]

    def trip(j, slot, state):
        new = []
        for h in range(2):
            mx, m, alpha, acc = state[h]
            mx_new = stage_a(h, slot, kblock(j, h))
            m, alpha_new = stage_b(h, 1 - slot, mx, m)
            acc = stage_c(h, slot, vblock(j - 2, h), alpha, acc)
            new.append((mx_new, m, alpha_new, acc))
        return tuple(new)

    zrow = jnp.zeros((1, tq), F32)
    neg = jnp.full((1, tq), NEG, F32)
    zacc = jnp.zeros((V_EXT, tq), F32)
    state = []
    for h in range(2):
        mx = stage_a(h, 0, k_ref[0:tk, sls[h]])
        if nk > 1:
            mx1 = stage_a(h, 1, k_ref[tk:2 * tk, sls[h]])
            m, alpha = stage_b(h, 0, mx, neg)
            state.append((mx1, m, alpha, zacc))
        else:
            state.append((mx, neg, zrow, zacc))
    state = tuple(state)
    if nk > 2:
        assert nk % 2 == 0

        def body(i, state):
            return trip(2 * i + 1, 1, trip(2 * i, 0, state))

        state = lax.fori_loop(1, nk // 2, body, state)
    outs = []
    for h in range(2):
        mx, m, alpha, acc = state[h]
        last = (nk - 1) % 2
        if nk > 1:
            acc = stage_c(h, 1 - last, v_ref[vsl[h], (nk - 2) * tk:(nk - 1) * tk], alpha, acc)
        m, alpha = stage_b(h, last, mx, m)
        acc = stage_c(h, last, v_ref[vsl[h], (nk - 1) * tk:nk * tk], alpha, acc)
        outs.append(acc[0:B_V] / acc[B_V:B_V + 1])
    o_ref[...] = jnp.concatenate(outs, axis=0).T.astype(o_ref.dtype)


def _key_chunk(tkeys):
    for c in (512, 384, 256, 128):
        if tkeys % c == 0 and (tkeys // c) % 2 == 0:
            return c
    raise ValueError(f"no key chunk for {tkeys} keys")


def _flash(q_t, k, v_t, tq, tk):
    b, _, t = q_t.shape
    tkeys = k.shape[1]
    tq = min(tq, t)
    tk = min(tk, tkeys)
    assert tkeys % tk == 0
    return pl.pallas_call(
        functools.partial(_flash_kernel, tk=tk, nk=tkeys // tk),
        grid=(b, B_HEADS // 2, t // tq),
        in_specs=[pl.BlockSpec((None, 2 * LANES, tq), lambda i, hp, j: (i, hp, j)),
                  pl.BlockSpec((None, tkeys, 2 * LANES), lambda i, hp, j: (i, 0, hp)),
                  pl.BlockSpec((None, 2 * V_EXT, tkeys), lambda i, hp, j: (i, hp, 0))],
        out_specs=pl.BlockSpec((None, tq, LANES), lambda i, hp, j: (i, j, hp)),
        out_shape=jax.ShapeDtypeStruct((b, t, B_WIDTH), BF16),
        scratch_shapes=[pltpu.VMEM((2, 2, tk, tq), F32), pltpu.VMEM((2, 2, tk, tq), BF16)],
        compiler_params=_cparams(("parallel", "parallel", "arbitrary")),
        name="flash",
    )(q_t, k, v_t)


NA_PAIR_ROWS = NA_KH + 2
NA_PAIR_KEYS = NA_PAIR_ROWS * GRID_W
NA_VARIANTS = 5


def _natten_pair_window(r, rows):
    return jnp.clip(r - NA_KH // 2, 0, rows - NA_PAIR_ROWS)


def _natten_kernel(qt_ref, k_ref, vt_ref, kc_ref, vct_ref, bias_ref, o_ref, *, rb, rows):
    blk = pl.program_id(2)
    tq = rb * GRID_W
    pq = 2 * GRID_W
    npair = rb // 2
    q_t = qt_ref[...]
    feat = lax.broadcasted_iota(jnp.int32, q_t.shape, 0)
    zero = jnp.zeros_like(q_t)
    qh = [jnp.where(feat < C_HEAD_DIM, q_t, zero), jnp.where(feat >= C_HEAD_DIM, q_t, zero)]
    q2 = [jnp.concatenate([qh[0][:, i * pq:(i + 1) * pq], qh[1][:, i * pq:(i + 1) * pq]], axis=1)
          for i in range(npair)]
    s_c = _dot(kc_ref[...], jnp.concatenate(q2, axis=1))
    m_c = jnp.max(s_c, axis=0, keepdims=True)

    def scores(i):
        r = blk * rb + 2 * i
        ws = _natten_pair_window(r, rows)
        koff = pl.multiple_of(ws * GRID_W, LANES)
        s_w = _dot(k_ref[pl.ds(koff, NA_PAIR_KEYS), :], q2[i])
        return koff, s_w + bias_ref[(r - ws) // 2]

    def softmax(i, s_w):
        csl = slice(i * 2 * pq, (i + 1) * 2 * pq)
        m = jnp.maximum(jnp.max(s_w, axis=0, keepdims=True), m_c[:, csl])
        p_w = jnp.exp2(s_w - m)
        p_ci = jnp.exp2(s_c[:, csl] - m)
        l = jnp.sum(p_w, axis=0, keepdims=True) + jnp.sum(p_ci, axis=0, keepdims=True)
        return p_w.astype(BF16), p_ci.astype(BF16), l

    o_w, p_c, l_all = [], [], []
    pending = scores(0)
    for i in range(npair):
        koff, s_w = pending
        if i + 1 < npair:
            pending = scores(i + 1)
        p_w, p_ci, l = softmax(i, s_w)
        p_c.append(p_ci)
        l_all.append(l)
        o_w.append(_dot(vt_ref[:, pl.ds(koff, NA_PAIR_KEYS)], p_w))
    o2 = ((jnp.concatenate(o_w, axis=1) + _dot(vct_ref[...], jnp.concatenate(p_c, axis=1)))
          / jnp.concatenate(l_all, axis=1))
    top = lax.broadcasted_iota(jnp.int32, (LANES, pq), 0) < C_HEAD_DIM
    out_t = jnp.concatenate(
        [jnp.where(top, o2[:, i * 2 * pq:i * 2 * pq + pq], o2[:, i * 2 * pq + pq:(i + 1) * 2 * pq])
         for i in range(npair)], axis=1)
    o_ref[...] = out_t.T.astype(o_ref.dtype)


def _natten(qv_t, k, kc, vc_t, bias_t, rb):
    b, t, _ = k.shape
    tc = kc.shape[1]
    rows = t // GRID_W
    rb = min(rb, rows)
    assert rows % 2 == 0 and rows >= NA_PAIR_ROWS + 4 and rb % 2 == 0
    nhp = C_HEADS // 2
    return pl.pallas_call(
        functools.partial(_natten_kernel, rb=rb, rows=rows),
        grid=(b, nhp, rows // rb),
        in_specs=[pl.BlockSpec((None, LANES, rb * GRID_W), lambda i, hp, j: (i, hp, j)),
                  pl.BlockSpec((None, t, LANES), lambda i, hp, j: (i, 0, hp)),
                  pl.BlockSpec((None, LANES, t), lambda i, hp, j: (i, nhp + hp, 0)),
                  pl.BlockSpec((None, tc, LANES), lambda i, hp, j: (i, 0, hp)),
                  pl.BlockSpec((None, LANES, tc), lambda i, hp, j: (i, hp, 0)),
                  pl.BlockSpec((None,) + bias_t.shape[1:], lambda i, hp, j: (hp, 0, 0, 0))],
        out_specs=pl.BlockSpec((None, rb * GRID_W, LANES), lambda i, hp, j: (i, j, hp)),
        out_shape=jax.ShapeDtypeStruct((b, t, C_HEADS * C_HEAD_DIM), BF16),
        compiler_params=_cparams(("parallel", "parallel", "arbitrary")),
        name="natten",
    )(qv_t, k, qv_t, kc, vc_t, bias_t)


def _natten_bias_table(rel_bias):
    qc = np.arange(GRID_W)[:, None]
    kcol = np.arange(GRID_W)[None, :]
    cs = np.clip(qc - NA_KW // 2, 0, GRID_W - NA_KW)
    valid = (kcol >= cs) & (kcol < cs + NA_KW)
    pad = GRID_W - NA_KW
    rbp = jnp.pad(rel_bias.astype(F32) * LOG2E, ((0, 0), (0, 0), (pad, pad)))
    toe = jnp.stack([rbp[:, :, GRID_W - 1 - q:2 * GRID_W - 1 - q] for q in range(GRID_W)], axis=2)
    toe = jnp.where(valid[None, None], toe, NEG)
    masked = jnp.full_like(toe[:, 0], NEG)
    own = ((0, 0), (0, 0), (0, 1), (2, 2), (2, 2))
    variants = []
    for v in range(NA_VARIANTS):
        halves = []
        for j in range(2):
            a = own[v][j]
            cols = [toe[:, x - 2 * v - j + NA_KH - 1] if a <= x < a + NA_KH else masked
                    for x in range(NA_PAIR_ROWS)]
            halves.append(jnp.concatenate(cols, axis=-1))
        variants.append(jnp.concatenate(halves, axis=1))
    tab = jnp.stack(variants, axis=1)
    tab = tab.reshape(C_HEADS // 2, 2, NA_VARIANTS, 2 * GRID_W, NA_PAIR_KEYS)
    return tab.transpose(0, 2, 4, 1, 3).reshape(C_HEADS // 2, NA_VARIANTS, NA_PAIR_KEYS, 4 * GRID_W)


def _mlp_tail(x1, g2, sh2, sc2, n2g, w1_ref, w2_ref):
    h = (_rms(x1) * n2g * (1.0 + sc2) + sh2).astype(BF16)
    acc = jnp.zeros_like(x1)
    chunk = 1024
    for c0 in range(0, D_FF, chunk):
        a = jnp.maximum(_dot(h, w1_ref[:, c0:c0 + chunk]), 0.0)
        acc = acc + _dot((a * a).astype(BF16), w2_ref[c0:c0 + chunk, :])
    return x1 + g2 * acc


def _post_ab_kernel(x_ref, hf_ref, hb_ref, og_ref, bl_ref, mg_ref, g1_ref, sh2_ref, sc2_ref,
                    g2_ref, n2g_ref, wo_ref, w1_ref, w2_ref, o_ref):
    hm = hf_ref[...] + hb_ref[...]
    og = jax.nn.sigmoid(og_ref[...].astype(F32))
    mg = mg_ref[...]
    y = _dot(bl_ref[...], wo_ref[M_WIDTH:, :])
    for h in range(M_HEADS):
        sl = slice(h * M_DV, (h + 1) * M_DV)
        xh = hm[sl]
        inv = lax.rsqrt(jnp.mean(xh * xh, axis=0, keepdims=True) + EPS)
        hn = (xh * inv * mg[sl] * og[sl]).astype(BF16)
        y = y + lax.dot_general(hn, wo_ref[sl, :], (((0,), (0,)), ((), ())),
                                preferred_element_type=F32)
    x1 = x_ref[...] + g1_ref[...] * y
    o_ref[...] = _mlp_tail(x1, g2_ref[...], sh2_ref[...], sc2_ref[...], n2g_ref[...],
                           w1_ref, w2_ref)


def _post_c_kernel(x_ref, y_ref, g1_ref, sh2_ref, sc2_ref, g2_ref, n2g_ref, fg_ref,
                   wo_ref, w1_ref, w2_ref, o_ref):
    x1 = x_ref[...] + g1_ref[...] * _dot(y_ref[...], wo_ref[...])
    x2 = _mlp_tail(x1, g2_ref[...], sh2_ref[...], sc2_ref[...], n2g_ref[...], w1_ref, w2_ref)
    o_ref[...] = _rms(x2) * fg_ref[...]


def _resident(a):
    return pl.BlockSpec(a.shape, lambda i, j: (0,) * a.ndim, pipeline_mode=pl.Buffered(1))


def _post_ab(x, h_dirs, qvo_t, b_lat, mg, mods, n2g, wo, w1, w2, tm):
    b, t, d = x.shape
    tm = min(tm, t)
    g1, sh2, sc2, g2 = mods
    bm = g1.shape[0]
    mod_map = (lambda i, j: (i, 0, 0)) if bm == b else (lambda i, j: (0, 0, 0))
    mod_spec = pl.BlockSpec((None, 1, d), mod_map)
    tok = lambda w, cb: pl.BlockSpec((None, tm, w), lambda i, j: (i, j, cb))
    return pl.pallas_call(
        _post_ab_kernel,
        grid=(b, t // tm),
        in_specs=[tok(d, 0),
                  pl.BlockSpec((None, None, M_WIDTH, tm), lambda i, j: (0, i, 0, j)),
                  pl.BlockSpec((None, None, M_WIDTH, tm), lambda i, j: (1, i, 0, j)),
                  pl.BlockSpec((None, M_WIDTH, tm), lambda i, j: (i, T_O // M_WIDTH, j)),
                  tok(B_WIDTH, 0),
                  pl.BlockSpec((M_WIDTH, 1), lambda i, j: (0, 0)),
                  mod_spec, mod_spec, mod_spec, mod_spec,
                  pl.BlockSpec((1, d), lambda i, j: (0, 0)),
                  _resident(wo), _resident(w1), _resident(w2)],
        out_specs=tok(d, 0),
        out_shape=jax.ShapeDtypeStruct((b, t, d), F32),
        compiler_params=_cparams(("parallel", "parallel")),
        name="post_ab",
    )(x, h_dirs, h_dirs, qvo_t, b_lat, mg, g1, sh2, sc2, g2, n2g.reshape(1, d), wo, w1, w2)


def _post_c(x, y, mods, n2g, fg, wo, w1, w2, tm):
    b, t, d = x.shape
    tm = min(tm, t)
    g1, sh2, sc2, g2 = mods
    mod_spec = pl.BlockSpec((None, 1, d), lambda i, j: (i, 0, 0))
    tok = pl.BlockSpec((None, tm, d), lambda i, j: (i, j, 0))
    vec = pl.BlockSpec((1, d), lambda i, j: (0, 0))
    return pl.pallas_call(
        _post_c_kernel,
        grid=(b, t // tm),
        in_specs=[tok, tok, mod_spec, mod_spec, mod_spec, mod_spec, vec, vec,
                  _resident(wo), _resident(w1), _resident(w2)],
        out_specs=tok,
        out_shape=jax.ShapeDtypeStruct((b, t, d), F32),
        compiler_params=_cparams(("parallel", "parallel")),
        name="post_c",
    )(x, y, g1, sh2, sc2, g2, n2g.reshape(1, d), fg.reshape(1, d), wo, w1, w2)


def _rope_tables(t):
    pos = np.arange(t)
    row = (pos // GRID_W).astype(np.float32)
    colp = (pos % GRID_W).astype(np.float32)
    n_f = B_ROPE // 4
    freqs = (ROPE_BASE ** (-np.arange(n_f, dtype=np.float32) / n_f)).astype(np.float32)
    ang = np.concatenate([row[:, None] * freqs, colp[:, None] * freqs], axis=-1)
    cos = np.zeros((t, LANES), np.float32)
    sin = np.zeros((t, LANES), np.float32)
    cos[:, :B_NOPE] = 1.0
    cos[:, B_NOPE:B_NOPE + 16] = np.cos(ang)
    cos[:, B_NOPE + 16:B_NOPE + 32] = np.cos(ang)
    sin[:, B_NOPE:B_NOPE + 16] = np.sin(ang)
    sin[:, B_NOPE + 16:B_NOPE + 32] = np.sin(ang)
    return tuple(jnp.asarray(a) for a in (cos, sin, cos.T.copy(), sin.T.copy()))


def _identity_tables(t):
    cos = np.zeros((t, LANES), np.float32)
    cos[:, :B_NOPE + B_ROPE] = 1.0
    sin = np.zeros((t, LANES), np.float32)
    return tuple(jnp.asarray(a) for a in (cos, sin, cos.T.copy(), sin.T.copy()))


def _layer0_weights(w_in, gate_b, w_uq, w_ukv):
    d = w_in.shape[0]
    half = B_ROPE // 2
    o_g = 4 * M_WIDTH
    o_cq = o_g + 4 * M_HEADS
    o_ckv = o_cq + B_Q_RANK
    o_kr = o_ckv + B_KV_RANK
    kr = w_in[:, o_kr:o_kr + B_ROPE]
    z = lambda n: jnp.zeros((d, n), w_in.dtype)
    kr1 = jnp.concatenate([z(B_NOPE), kr, z(LANES - B_NOPE - B_ROPE)], axis=1)
    kr2 = jnp.concatenate([z(B_NOPE), -kr[:, half:], kr[:, :half], z(LANES - B_NOPE - B_ROPE)], axis=1)
    wg = w_in[:, o_g:o_g + 4 * M_HEADS]
    gpad = z(LANES - 2 * M_HEADS)
    mq, mk, mv, mo = (w_in[:, i * M_WIDTH:(i + 1) * M_WIDTH] for i in range(4))
    w0 = jnp.concatenate([mk, w_in[:, o_cq:o_kr], kr1, kr2,
                          wg[:, :2 * M_HEADS], gpad, wg[:, 2 * M_HEADS:], gpad], axis=1).astype(BF16)
    w0_t = jnp.concatenate([mq, mv, mo], axis=1).T.astype(BF16)
    gb = jnp.zeros((2, 1, LANES), F32)
    gb = gb.at[0, 0, :2 * M_HEADS].set(gate_b[:2 * M_HEADS]).at[1, 0, :2 * M_HEADS].set(gate_b[2 * M_HEADS:])

    qr = B_Q_RANK
    wq = w_uq.reshape(qr, B_HEADS, B_NOPE + B_ROPE)
    nope, ra, rb = wq[..., :B_NOPE], wq[..., B_NOPE:B_NOPE + half], wq[..., B_NOPE + half:]
    zq = jnp.zeros((qr, B_HEADS, LANES - B_NOPE - B_ROPE), w_uq.dtype)
    wq1 = jnp.concatenate([nope, ra, rb, zq], axis=-1).reshape(qr, B_HEADS * LANES).astype(BF16)
    wq2 = jnp.concatenate([jnp.zeros_like(nope), -rb, ra, zq], axis=-1).reshape(qr, B_HEADS * LANES).astype(BF16)
    wkv = w_ukv.reshape(B_KV_RANK, B_HEADS, B_NOPE + B_V)
    wk = jnp.concatenate([wkv[..., :B_NOPE], jnp.zeros((B_KV_RANK, B_HEADS, LANES - B_NOPE), w_ukv.dtype)],
                         axis=-1).reshape(B_KV_RANK, B_HEADS * LANES).astype(BF16)
    wv = wkv[..., B_NOPE:].reshape(B_KV_RANK, B_WIDTH).astype(BF16)
    return w0, w0_t, gb, wq1.T, wq2.T, wk, wv.T


def _split_mod(mod_rows):
    return [m[:, None, :] for m in jnp.split(mod_rows, 6, axis=-1)]


def kernel(x, c, ctx, c_ctx, ada_w, ada_b, norm1_g, norm2_g, mlp_w1, mlp_w2, ab_w_in, ab_gate_b,
           ab_m_norm_g, ab_q_norm_g, ab_kv_norm_g, ab_w_uq, ab_w_ukv, ab_w_out, na_w_in,
           na_rel_bias, na_w_out, final_norm_g):
    b, t, d = x.shape
    tc = ctx.shape[1]
    c_rows = jnp.concatenate([c, c_ctx[None, :], jnp.zeros((8 - b - 1, d), F32)], axis=0)
    mod = _ada(c_rows, ada_w, ada_b)

    sh1, sc1, g1, sh2, sc2, g2 = _split_mod(mod[0, :b])
    csh1, csc1, cg1, csh2, csc2, cg2 = _split_mod(mod[0, b:b + 1])
    w0, w0_t, gb, wq1, wq2, wk, wv = _layer0_weights(ab_w_in[0], ab_gate_b[0], ab_w_uq[0],
                                                      ab_w_ukv[0])
    widths, dtypes = (P_MAIN, P_GATES), (BF16, F32)
    p_main, p_gates, qvo_t = _modproj(x, sh1, sc1, norm1_g[0], w0, widths, dtypes, tm=512, w_t=w0_t)
    pc_main, pc_gates, qvo_ct = _modproj(ctx, csh1, csc1, norm1_g[0], w0, widths, dtypes, tm=256,
                                         w_t=w0_t)

    hc_dirs, st_c, st_m = _mlstm(pc_main, pc_gates, qvo_ct, gb, None, True)
    (h_dirs,) = _mlstm(p_main, p_gates, qvo_t, gb, (st_c, st_m), False)

    qg = ab_q_norm_g[0].reshape(1, -1)
    kvg = ab_kv_norm_g[0].reshape(1, -1)
    q_l, k_l, v_l = _mlaprep(p_main, _rope_tables(t), qg, kvg, wq1, wq2, wk, wv, tm=512)
    q_c, k_c, v_c = _mlaprep(pc_main, _identity_tables(tc), qg, kvg, wq1, wq2, wk, wv, tm=256)
    k_all = jnp.concatenate([k_l, k_c], axis=1)
    v_all = jnp.concatenate([v_l, v_c], axis=2)
    b_lat = _flash(q_l, k_all, v_all, tq=2048, tk=_key_chunk(t + tc))
    b_ctx = _flash(q_c, k_c, v_c, tq=256, tk=tc)

    wo = ab_w_out[0].astype(BF16)
    w1 = mlp_w1[0].astype(BF16)
    w2 = mlp_w2[0].astype(BF16)
    mg = ab_m_norm_g[0].reshape(M_WIDTH, 1)
    x = _post_ab(x, h_dirs, qvo_t, b_lat, mg, (g1, sh2, sc2, g2), norm2_g[0], wo, w1, w2, tm=512)
    ctx = _post_ab(ctx, hc_dirs, qvo_ct, b_ctx, mg, (cg1, csh2, csc2, cg2), norm2_g[0],
                   wo, w1, w2, tm=256)

    sh1, sc1, g1, sh2, sc2, g2 = _split_mod(mod[1, :b])
    csh1, csc1 = _split_mod(mod[1, b:b + 1])[:2]
    wn = na_w_in[0].astype(BF16)
    cw = C_HEADS * C_HEAD_DIM
    w_q = (na_w_in[0][:, :cw] * (C_HEAD_DIM ** -0.5 * LOG2E)).astype(BF16)
    w_k = wn[:, cw:2 * cw]
    w_v = wn[:, 2 * cw:]
    k_n, qv_t = _modproj(x, sh1, sc1, norm1_g[1], w_k, (cw,), (BF16,), tm=512,
                         w_t=jnp.concatenate([w_q, w_v], axis=1).T)
    kc_n, vc_t = _modproj(ctx, csh1, csc1, norm1_g[1], w_k, (cw,), (BF16,), tm=256, w_t=w_v.T)
    bias_t = _natten_bias_table(na_rel_bias[0])
    y = _natten(qv_t, k_n, kc_n, vc_t, bias_t, rb=16)
    return _post_c(x, y, (g1, sh2, sc2, g2), norm2_g[1], final_norm_g,
                   na_w_out[0].astype(BF16), mlp_w1[1].astype(BF16), mlp_w2[1].astype(BF16), tm=512)
```

```python
import functools

import numpy as np
import jax
import jax.numpy as jnp
from jax import lax
from jax.experimental import pallas as pl
from jax.experimental.pallas import tpu as pltpu

F32 = jnp.float32
BF16 = jnp.bfloat16

D_MODEL = 1024
DEPTH = 2
GRID_W = 64
EPS = 1e-6
M_HEADS = 4
M_DK = 128
M_DV = 128
B_HEADS = 8
B_Q_RANK = 256
B_KV_RANK = 128
B_NOPE = 64
B_ROPE = 32
B_V = 64
ROPE_BASE = 10000.0
C_HEADS = 16
C_HEAD_DIM = 64
NA_KH = 8
NA_KW = 16
D_FF = 4 * D_MODEL
M_WIDTH = M_HEADS * M_DV
B_WIDTH = B_HEADS * B_V

LANES = 128
NEG = -1e30
LOG2E = float(np.log2(np.e))
VMEM_LIMIT = 56 * 1024 * 1024

P_K, P_CQ, P_CKV, P_KR1, P_KR2 = 0, 512, 768, 896, 1024
P_MAIN = 1152
T_Q, T_V, T_O = 0, 512, 1024
P_GATES = 256
MLSTM_CHUNK = 256
NV = 2 * M_DV
BF16_ROWS = 16
V_EXT = B_V + BF16_ROWS


def _cparams(sem):
    return pltpu.CompilerParams(dimension_semantics=sem, vmem_limit_bytes=VMEM_LIMIT)


def _rms(x):
    return x * lax.rsqrt(jnp.mean(x * x, axis=-1, keepdims=True) + EPS)


def _dot(a, b):
    return jnp.dot(a, b, preferred_element_type=F32)


def _dot_nt(a, b):
    return lax.dot_general(a, b, (((1,), (1,)), ((), ())), preferred_element_type=F32)


def _ada_kernel(c_ref, w_ref, b_ref, o_ref):
    c = c_ref[...]
    s = c * jax.nn.sigmoid(c)
    o_ref[...] = jnp.dot(s, w_ref[...], precision=lax.Precision.HIGHEST,
                         preferred_element_type=F32) + b_ref[...]


def _ada(c_rows, ada_w, ada_b):
    depth, d, n = ada_w.shape
    tn = 1536
    return pl.pallas_call(
        _ada_kernel,
        grid=(depth, n // tn),
        in_specs=[pl.BlockSpec((8, d), lambda l, j: (0, 0)),
                  pl.BlockSpec((None, d, tn), lambda l, j: (l, 0, j)),
                  pl.BlockSpec((None, 1, tn), lambda l, j: (l, 0, j))],
        out_specs=pl.BlockSpec((None, 8, tn), lambda l, j: (l, 0, j)),
        out_shape=jax.ShapeDtypeStruct((depth, 8, n), F32),
        compiler_params=_cparams(("arbitrary", "arbitrary")),
        name="ada",
    )(c_rows, ada_w, ada_b.reshape(depth, 1, n))


def _modproj_kernel(x_ref, sh_ref, sc_ref, g_ref, w_ref, *refs, widths, has_t, chunk):
    wt_ref = refs[0] if has_t else None
    o_refs = refs[1:] if has_t else refs
    x = x_ref[...]
    h = _rms(x) * g_ref[...] * (1.0 + sc_ref[...]) + sh_ref[...]
    hb = h.astype(BF16)
    off = 0
    for o_ref, n in zip(o_refs, widths):
        for c0 in range(0, n, chunk):
            c1 = min(n, c0 + chunk)
            o_ref[:, c0:c1] = _dot(hb, w_ref[:, off + c0:off + c1]).astype(o_ref.dtype)
        off += n
    if has_t:
        ot_ref = o_refs[-1]
        for c0 in range(0, wt_ref.shape[0], chunk):
            ot_ref[c0:c0 + chunk, :] = _dot_nt(wt_ref[c0:c0 + chunk, :], hb).astype(ot_ref.dtype)


def _modproj(x, shift, scale, g, w, widths, dtypes, tm, w_t=None):
    b, t, d = x.shape
    bm = shift.shape[0]
    mod_map = (lambda i, j: (i, 0, 0)) if bm == b else (lambda i, j: (0, 0, 0))
    tm = min(tm, t)
    in_specs = [pl.BlockSpec((None, tm, d), lambda i, j: (i, j, 0)),
                pl.BlockSpec((None, 1, d), mod_map),
                pl.BlockSpec((None, 1, d), mod_map),
                pl.BlockSpec((1, d), lambda i, j: (0, 0)),
                pl.BlockSpec(w.shape, lambda i, j: (0, 0))]
    args = [x, shift, scale, g.reshape(1, d), w]
    out_specs = [pl.BlockSpec((None, tm, n), lambda i, j: (i, j, 0)) for n in widths]
    out_shape = [jax.ShapeDtypeStruct((b, t, n), dt) for n, dt in zip(widths, dtypes)]
    if w_t is not None:
        in_specs.append(pl.BlockSpec(w_t.shape, lambda i, j: (0, 0)))
        args.append(w_t)
        out_specs.append(pl.BlockSpec((None, w_t.shape[0], tm), lambda i, j: (i, 0, j)))
        out_shape.append(jax.ShapeDtypeStruct((b, w_t.shape[0], t), BF16))
    return pl.pallas_call(
        functools.partial(_modproj_kernel, widths=widths, has_t=w_t is not None, chunk=512),
        grid=(b, t // tm),
        in_specs=in_specs,
        out_specs=out_specs,
        out_shape=out_shape,
        compiler_params=_cparams(("parallel", "parallel")),
        name="modproj",
    )(*args)


def _mlstm_kernel(*refs, has_init, emit_state):
    qt_ref, k_ref, vt_ref, g_ref, gb_ref = refs[:5]
    pos = 5
    if has_init:
        c0_ref, m0_ref = refs[pos:pos + 2]
        pos += 2
    h_ref = refs[pos]
    pos += 1
    if emit_state:
        cout_ref, mout_ref = refs[pos:pos + 2]
        pos += 2
    c_s, m_s = refs[pos:pos + 2]

    d = pl.program_id(1)
    j = pl.program_id(2)
    L = k_ref.shape[0]

    @pl.when(j == 0)
    def _():
        if has_init:
            c_s[...] = c0_ref[...]
            m_s[...] = m0_ref[...]
        else:
            c_s[...] = jnp.zeros_like(c_s)
            m_s[...] = jnp.zeros_like(m_s)

    r = lax.broadcasted_iota(jnp.int32, (L, L), 0)
    c = lax.broadcasted_iota(jnp.int32, (L, L), 1)
    sgn = 1 - 2 * d
    keep = (r - c) * sgn >= 0
    keep_t = (c - r) * sgn >= 0

    gates = g_ref[...] + gb_ref[...]
    ls = jax.nn.log_sigmoid(gates)
    gates_t = gates.T[0:8]
    ls_t = jax.nn.log_sigmoid(gates_t)

    def split3(a):
        hi = a.astype(BF16)
        mid = (a - hi.astype(F32)).astype(BF16)
        lo = (a - hi.astype(F32) - mid.astype(F32)).astype(BF16)
        return hi, mid, lo

    trib = keep.astype(BF16)
    bc3 = _dot(trib, jnp.concatenate(split3(ls), axis=1))
    b_cols = bc3[:, 0:LANES] + bc3[:, LANES:2 * LANES] + bc3[:, 2 * LANES:]
    br3 = _dot_nt(jnp.concatenate(split3(ls_t), axis=0), trib)
    b_rows = br3[0:8] + br3[8:16] + br3[16:24]
    b_tot = jnp.sum(ls, axis=0, keepdims=True)

    ones_rows = jnp.ones((M_DV, L), BF16)
    scale = M_DK ** -0.5
    hsl = [slice(h * M_DK, (h + 1) * M_DK) for h in range(M_HEADS)]

    cn, m, s_raw, qc = [], [], [], []
    for h in range(M_HEADS):
        cn.append(c_s[h])
        m.append(m_s[h][0:1, 0:1])
        lhs = jnp.concatenate([k_ref[:, hsl[h]], cn[h].astype(BF16)], axis=0)
        sq = _dot(lhs, qt_ref[hsl[h], :])
        s_raw.append(sq[0:L])
        qc.append(sq[L:])

    s_w, w_inter, m_t, vext, vw, decay, m_new = [], [], [], [], [], [], []
    for h in range(M_HEADS):
        col = gates[:, h:h + 1] - b_cols[:, 4 + h:5 + h]
        li_r = gates_t[h:h + 1, :]
        b_r = b_rows[4 + h:5 + h, :]
        b_last = b_tot[:, 4 + h:5 + h]
        dm = jnp.where(keep_t, b_r + col, NEG)
        inter = b_r + m[h]
        m_t.append(jnp.maximum(jnp.max(dm, axis=0, keepdims=True), inter))
        s_w.append((s_raw[h] * (scale * jnp.exp(dm - m_t[h]))).astype(BF16))
        w_inter.append(jnp.exp(inter - m_t[h]) * scale)
        g_r = b_last - b_r + li_r
        m_new.append(jnp.maximum(b_last + m[h], jnp.max(g_r, axis=1, keepdims=True)))
        decay.append(jnp.exp(b_last + m[h] - m_new[h]))
        vext.append(jnp.concatenate([vt_ref[hsl[h], :], ones_rows], axis=0))
        vw.append((vext[h].astype(F32) * jnp.exp(g_r - m_new[h])).astype(BF16))

    for h in range(M_HEADS):
        nd = w_inter[h] * qc[h] + _dot(vext[h], s_w[h])
        den = nd[M_DV:M_DV + 1]
        h_ref[hsl[h], :] = nd[:M_DV] / jnp.maximum(jnp.abs(den), jnp.exp(-m_t[h]))
        c_s[h] = decay[h] * cn[h] + _dot(vw[h], k_ref[:, hsl[h]])
        m_s[h] = jnp.broadcast_to(m_new[h], m_s.shape[1:])

    if emit_state:
        cout_ref[...] = c_s[...]
        mout_ref[...] = m_s[...]


def _mlstm(p_main, p_gates, qvo_t, gate_b2, init, emit_state):
    b, t, _ = p_main.shape
    L = min(MLSTM_CHUNK, t)
    nc = t // L

    def cidx(d, j):
        return j + d * (nc - 1 - 2 * j)

    def rows(rb):
        return pl.BlockSpec((None, M_WIDTH, L), lambda i, d, j: (i, rb, cidx(d, j)))

    in_specs = [rows(T_Q // M_WIDTH),
                pl.BlockSpec((None, L, M_WIDTH), lambda i, d, j: (i, cidx(d, j), P_K // M_WIDTH)),
                rows(T_V // M_WIDTH),
                pl.BlockSpec((None, L, LANES), lambda i, d, j: (i, cidx(d, j), d)),
                pl.BlockSpec((None, 1, LANES), lambda i, d, j: (d, 0, 0))]
    args = [qvo_t, p_main, qvo_t, p_gates, gate_b2]
    st_c = pl.BlockSpec((None, None, M_HEADS, NV, M_DK), lambda i, d, j: (i, d, 0, 0, 0))
    st_m = pl.BlockSpec((None, None, M_HEADS, 8, LANES), lambda i, d, j: (i, d, 0, 0, 0))
    if init is not None:
        in_specs += [st_c, st_m]
        args += list(init)
    out_specs = [pl.BlockSpec((None, None, M_WIDTH, L), lambda i, d, j: (d, i, 0, cidx(d, j)))]
    out_shape = [jax.ShapeDtypeStruct((2, b, M_WIDTH, t), F32)]
    if emit_state:
        out_specs += [st_c, st_m]
        out_shape += [jax.ShapeDtypeStruct((b, 2, M_HEADS, NV, M_DK), F32),
                      jax.ShapeDtypeStruct((b, 2, M_HEADS, 8, LANES), F32)]
    return pl.pallas_call(
        functools.partial(_mlstm_kernel, has_init=init is not None, emit_state=emit_state),
        grid=(b, 2, nc),
        in_specs=in_specs,
        out_specs=out_specs,
        out_shape=out_shape,
        scratch_shapes=[pltpu.VMEM((M_HEADS, NV, M_DK), F32),
                        pltpu.VMEM((M_HEADS, 8, LANES), F32)],
        compiler_params=_cparams(("parallel", "parallel", "arbitrary")),
        name="mlstm",
    )(*args)


def _mlaprep_kernel(cq_ref, ckv_ref, kr1_ref, kr2_ref, cos_ref, sin_ref, cost_ref, sint_ref,
                    qg_ref, kvg_ref, wq1_ref, wq2_ref, wk_ref, wv_ref, qt_ref, k_ref, vt_ref):
    cos = cos_ref[...]
    sin = sin_ref[...]
    cq = cq_ref[...].astype(F32)
    cqn = (_rms(cq) * qg_ref[...]).astype(BF16)
    ckv = ckv_ref[...].astype(F32)
    ckvn = (_rms(ckv) * kvg_ref[...]).astype(BF16)
    a_scale = (B_NOPE + B_ROPE) ** -0.5 * LOG2E
    kr = kr1_ref[...].astype(F32) * cos + kr2_ref[...].astype(F32) * sin
    cos_t = cost_ref[...] * a_scale
    sin_t = sint_ref[...] * a_scale
    ones = jnp.ones((V_EXT - B_V, vt_ref.shape[1]), vt_ref.dtype)
    for c0 in range(0, B_WIDTH, LANES):
        v2 = _dot_nt(wv_ref[c0:c0 + LANES, :], ckvn).astype(vt_ref.dtype)
        for i in range(LANES // B_V):
            r0 = (c0 // B_V + i) * V_EXT
            vt_ref[r0:r0 + B_V, :] = v2[i * B_V:(i + 1) * B_V]
            vt_ref[r0 + B_V:r0 + V_EXT, :] = ones
    for h in range(B_HEADS):
        sl = slice(h * LANES, (h + 1) * LANES)
        q1 = _dot_nt(wq1_ref[sl, :], cqn)
        q2 = _dot_nt(wq2_ref[sl, :], cqn)
        qt_ref[sl, :] = (q1 * cos_t + q2 * sin_t).astype(qt_ref.dtype)
        k_ref[:, sl] = (_dot(ckvn, wk_ref[:, sl]) + kr).astype(k_ref.dtype)


def _mlaprep(p_main, tables, qg, kvg, wq1, wq2, wk, wv, tm):
    cos, sin, cos_t, sin_t = tables
    b, t, _ = p_main.shape
    tm = min(tm, t)
    hw = B_HEADS * LANES

    def tok(width, cb):
        return pl.BlockSpec((None, tm, width), lambda i, j: (i, j, cb))

    def full(a):
        return pl.BlockSpec(a.shape, lambda i, j: (0,) * a.ndim)

    return pl.pallas_call(
        _mlaprep_kernel,
        grid=(b, t // tm),
        in_specs=[tok(B_Q_RANK, P_CQ // B_Q_RANK), tok(LANES, P_CKV // LANES),
                  tok(LANES, P_KR1 // LANES), tok(LANES, P_KR2 // LANES),
                  pl.BlockSpec((tm, LANES), lambda i, j: (j, 0)),
                  pl.BlockSpec((tm, LANES), lambda i, j: (j, 0)),
                  pl.BlockSpec((LANES, tm), lambda i, j: (0, j)),
                  pl.BlockSpec((LANES, tm), lambda i, j: (0, j)),
                  full(qg), full(kvg), full(wq1), full(wq2), full(wk), full(wv)],
        out_specs=[pl.BlockSpec((None, hw, tm), lambda i, j: (i, 0, j)),
                   pl.BlockSpec((None, tm, hw), lambda i, j: (i, j, 0)),
                   pl.BlockSpec((None, B_HEADS * V_EXT, tm), lambda i, j: (i, 0, j))],
        out_shape=[jax.ShapeDtypeStruct((b, hw, t), BF16),
                   jax.ShapeDtypeStruct((b, t, hw), BF16),
                   jax.ShapeDtypeStruct((b, B_HEADS * V_EXT, t), BF16)],
        compiler_params=_cparams(("parallel", "parallel")),
        name="mlaprep",
    )(p_main, p_main, p_main, p_main, cos, sin, cos_t, sin_t, qg, kvg, wq1, wq2, wk, wv)


def _flash_kernel(*refs, tk, nk, has_ctx):
    if has_ctx:
        q_ref, k_ref, v_ref, kc_ref, vc_ref, o_ref, s_s, p_s = refs
    else:
        q_ref, k_ref, v_ref, o_ref, s_s, p_s = refs
    tq = q_ref.shape[1]
    sls = [slice(h * LANES, (h + 1) * LANES) for h in range(2)]
    vsl = [slice(h * V_EXT, (h + 1) * V_EXT) for h in range(2)]

    def stage_a(h, slot, kb):
        s = _dot(kb, q_ref[sls[h], :])
        s_s[h, slot, 0:kb.shape[0]] = s
        return jnp.max(s, axis=0, keepdims=True)

    def stage_b(h, slot, rows, mx, m):
        m_new = jnp.maximum(m, mx)
        p_s[h, slot, 0:rows] = jnp.exp2(s_s[h, slot, 0:rows] - m_new).astype(BF16)
        return m_new, jnp.exp2(m - m_new)

    def stage_c(h, slot, vb, alpha, acc):
        return alpha * acc + _dot(vb, p_s[h, slot, 0:vb.shape[1]])

    def kblock(j, h):
        return k_ref[pl.ds(pl.multiple_of(j * tk, tk), tk), sls[h]]

    def vblock(j, h):
        return v_ref[vsl[h], pl.ds(pl.multiple_of(j * tk, LANES), tk)]

    def trip(slot, state, k_new, v_old):
        new = []
        for h in range(2):
            mx, m, alpha, acc = state[h]
            mx_new = stage_a(h, slot, k_new(h))
            m, alpha_new = stage_b(h, 1 - slot, tk, mx, m)
            acc = stage_c(h, slot, v_old(h), alpha, acc)
            new.append((mx_new, m, alpha_new, acc))
        return tuple(new)

    zrow = jnp.zeros((1, tq), F32)
    neg = jnp.full((1, tq), NEG, F32)
    zacc = jnp.zeros((V_EXT, tq), F32)
    state = []
    for h in range(2):
        mx = stage_a(h, 0, k_ref[0:tk, sls[h]])
        if nk > 1:
            mx1 = stage_a(h, 1, k_ref[tk:2 * tk, sls[h]])
            m, alpha = stage_b(h, 0, tk, mx, neg)
            state.append((mx1, m, alpha, zacc))
        else:
            state.append((mx, neg, zrow, zacc))
    state = tuple(state)
    if nk > 2:
        assert nk % 2 == 0

        def body(i, state):
            for par in range(2):
                j = 2 * i + par
                state = trip(par, state, functools.partial(kblock, j),
                             functools.partial(vblock, j - 2))
            return state

        state = lax.fori_loop(1, nk // 2, body, state)
    last = (nk - 1) % 2
    tail_rows = tk
    if has_ctx:
        assert nk > 1 and kc_ref.shape[0] <= tk
        state = trip(1 - last, state, lambda h: kc_ref[:, sls[h]],
                     lambda h: v_ref[vsl[h], (nk - 2) * tk:(nk - 1) * tk])
        tail_rows = kc_ref.shape[0]
    outs = []
    for h in range(2):
        mx, m, alpha, acc = state[h]
        if has_ctx:
            acc = stage_c(h, last, v_ref[vsl[h], (nk - 1) * tk:nk * tk], alpha, acc)
            m, alpha = stage_b(h, 1 - last, tail_rows, mx, m)
            acc = stage_c(h, 1 - last, vc_ref[vsl[h], :], alpha, acc)
        else:
            if nk > 1:
                acc = stage_c(h, 1 - last, v_ref[vsl[h], (nk - 2) * tk:(nk - 1) * tk], alpha, acc)
            m, alpha = stage_b(h, last, tail_rows, mx, m)
            acc = stage_c(h, last, v_ref[vsl[h], (nk - 1) * tk:nk * tk], alpha, acc)
        outs.append(acc[0:B_V] / acc[B_V:B_V + 1])
    o_ref[...] = jnp.concatenate(outs, axis=0).T.astype(o_ref.dtype)


def _flash(q_t, k, v_t, kc, vc_t, tq, tk):
    b, _, t = q_t.shape
    tkeys = k.shape[1]
    tq = min(tq, t)
    tk = min(tk, tkeys)
    assert tkeys % tk == 0
    has_ctx = kc is not None
    in_specs = [pl.BlockSpec((None, 2 * LANES, tq), lambda i, hp, j: (i, hp, j)),
                pl.BlockSpec((None, tkeys, 2 * LANES), lambda i, hp, j: (i, 0, hp)),
                pl.BlockSpec((None, 2 * V_EXT, tkeys), lambda i, hp, j: (i, hp, 0))]
    args = [q_t, k, v_t]
    if has_ctx:
        tc = kc.shape[1]
        in_specs += [pl.BlockSpec((None, tc, 2 * LANES), lambda i, hp, j: (i, 0, hp)),
                     pl.BlockSpec((None, 2 * V_EXT, tc), lambda i, hp, j: (i, hp, 0))]
        args += [kc, vc_t]
    return pl.pallas_call(
        functools.partial(_flash_kernel, tk=tk, nk=tkeys // tk, has_ctx=has_ctx),
        grid=(b, B_HEADS // 2, t // tq),
        in_specs=in_specs,
        out_specs=pl.BlockSpec((None, tq, LANES), lambda i, hp, j: (i, j, hp)),
        out_shape=jax.ShapeDtypeStruct((b, t, B_WIDTH), BF16),
        scratch_shapes=[pltpu.VMEM((2, 2, tk, tq), F32), pltpu.VMEM((2, 2, tk, tq), BF16)],
        compiler_params=_cparams(("parallel", "parallel", "arbitrary")),
        name="flash",
    )(*args)


NA_PAIR_ROWS = NA_KH + 2
NA_PAIR_KEYS = NA_PAIR_ROWS * GRID_W
NA_VARIANTS = 5


def _natten_pair_window(r, rows):
    return jnp.clip(r - NA_KH // 2, 0, rows - NA_PAIR_ROWS)


def _natten_kernel(qt_ref, k_ref, vt_ref, kc_ref, vct_ref, bias_ref, o_ref, *, rb, rows):
    blk = pl.program_id(2)
    tq = rb * GRID_W
    pq = 2 * GRID_W
    npair = rb // 2
    q_t = qt_ref[...]
    feat = lax.broadcasted_iota(jnp.int32, q_t.shape, 0)
    zero = jnp.zeros_like(q_t)
    qh = [jnp.where(feat < C_HEAD_DIM, q_t, zero), jnp.where(feat >= C_HEAD_DIM, q_t, zero)]
    q2 = [jnp.concatenate([qh[0][:, i * pq:(i + 1) * pq], qh[1][:, i * pq:(i + 1) * pq]], axis=1)
          for i in range(npair)]
    s_c = _dot(kc_ref[...], jnp.concatenate(q2, axis=1))
    m_c = jnp.max(s_c, axis=0, keepdims=True)

    def scores(i):
        r = blk * rb + 2 * i
        ws = _natten_pair_window(r, rows)
        koff = pl.multiple_of(ws * GRID_W, LANES)
        s_w = _dot(k_ref[pl.ds(koff, NA_PAIR_KEYS), :], q2[i])
        return koff, s_w + bias_ref[(r - ws) // 2]

    def softmax(i, s_w):
        csl = slice(i * 2 * pq, (i + 1) * 2 * pq)
        m = jnp.maximum(jnp.max(s_w, axis=0, keepdims=True), m_c[:, csl])
        p_w = jnp.exp2(s_w - m)
        p_ci = jnp.exp2(s_c[:, csl] - m)
        l = jnp.sum(p_w, axis=0, keepdims=True) + jnp.sum(p_ci, axis=0, keepdims=True)
        return p_w.astype(BF16), p_ci.astype(BF16), l

    o_w, p_c, l_all = [], [], []
    pending = scores(0)
    for i in range(npair):
        koff, s_w = pending
        if i + 1 < npair:
            pending = scores(i + 1)
        p_w, p_ci, l = softmax(i, s_w)
        p_c.append(p_ci)
        l_all.append(l)
        o_w.append(_dot(vt_ref[:, pl.ds(koff, NA_PAIR_KEYS)], p_w))
    o2 = ((jnp.concatenate(o_w, axis=1) + _dot(vct_ref[...], jnp.concatenate(p_c, axis=1)))
          / jnp.concatenate(l_all, axis=1))
    top = lax.broadcasted_iota(jnp.int32, (LANES, pq), 0) < C_HEAD_DIM
    out_t = jnp.concatenate(
        [jnp.where(top, o2[:, i * 2 * pq:i * 2 * pq + pq], o2[:, i * 2 * pq + pq:(i + 1) * 2 * pq])
         for i in range(npair)], axis=1)
    o_ref[...] = out_t.T.astype(o_ref.dtype)


def _natten(qv_t, k, kc, vc_t, bias_t, rb):
    b, t, _ = k.shape
    tc = kc.shape[1]
    rows = t // GRID_W
    rb = min(rb, rows)
    assert rows % 2 == 0 and rows >= NA_PAIR_ROWS + 4 and rb % 2 == 0
    nhp = C_HEADS // 2
    return pl.pallas_call(
        functools.partial(_natten_kernel, rb=rb, rows=rows),
        grid=(b, nhp, rows // rb),
        in_specs=[pl.BlockSpec((None, LANES, rb * GRID_W), lambda i, hp, j: (i, hp, j)),
                  pl.BlockSpec((None, t, LANES), lambda i, hp, j: (i, 0, hp)),
                  pl.BlockSpec((None, LANES, t), lambda i, hp, j: (i, nhp + hp, 0)),
                  pl.BlockSpec((None, tc, LANES), lambda i, hp, j: (i, 0, hp)),
                  pl.BlockSpec((None, LANES, tc), lambda i, hp, j: (i, hp, 0)),
                  pl.BlockSpec((None,) + bias_t.shape[1:], lambda i, hp, j: (hp, 0, 0, 0))],
        out_specs=pl.BlockSpec((None, rb * GRID_W, LANES), lambda i, hp, j: (i, j, hp)),
        out_shape=jax.ShapeDtypeStruct((b, t, C_HEADS * C_HEAD_DIM), BF16),
        compiler_params=_cparams(("parallel", "parallel", "arbitrary")),
        name="natten",
    )(qv_t, k, qv_t, kc, vc_t, bias_t)


def _natten_bias_table(rel_bias):
    kcol = np.arange(GRID_W)[:, None]
    qc = np.arange(GRID_W)[None, :]
    cs = np.clip(qc - NA_KW // 2, 0, GRID_W - NA_KW)
    valid = (kcol >= cs) & (kcol < cs + NA_KW)
    pad = GRID_W - NA_KW
    rev = jnp.pad(rel_bias.astype(F32) * LOG2E, ((0, 0), (0, 0), (pad, pad)))[:, :, ::-1]
    toe = jnp.stack([rev[:, :, GRID_W - 1 - k:2 * GRID_W - 1 - k] for k in range(GRID_W)], axis=2)
    toe = jnp.where(valid[None, None], toe, NEG)
    nhp = C_HEADS // 2
    toe = toe.reshape(nhp, 2, 2 * NA_KH - 1, GRID_W, GRID_W).transpose(0, 2, 3, 1, 4)
    masked = jnp.full_like(toe[:, 0], NEG)
    own = ((0, 0), (0, 0), (0, 1), (2, 2), (2, 2))
    variants = []
    for v in range(NA_VARIANTS):
        key_rows = []
        for x in range(NA_PAIR_ROWS):
            halves = [toe[:, x - 2 * v - j + NA_KH - 1] if own[v][j] <= x < own[v][j] + NA_KH
                      else masked for j in range(2)]
            key_rows.append(jnp.stack(halves, axis=3))
        variants.append(jnp.stack(key_rows, axis=1))
    return jnp.stack(variants, axis=1).reshape(nhp, NA_VARIANTS, NA_PAIR_KEYS, 4 * GRID_W)


def _mlp_tail(x1, g2, sh2, sc2, n2g, w1_ref, w2_ref):
    h = (_rms(x1) * n2g * (1.0 + sc2) + sh2).astype(BF16)
    acc = jnp.zeros_like(x1)
    chunk = 1024
    for c0 in range(0, D_FF, chunk):
        a = jnp.maximum(_dot(h, w1_ref[:, c0:c0 + chunk]), 0.0)
        acc = acc + _dot((a * a).astype(BF16), w2_ref[c0:c0 + chunk, :])
    return x1 + g2 * acc


def _post_ab_kernel(x_ref, hf_ref, hb_ref, og_ref, bl_ref, mg_ref, g1_ref, sh2_ref, sc2_ref,
                    g2_ref, n2g_ref, wo_ref, w1_ref, w2_ref, o_ref):
    hm = hf_ref[...] + hb_ref[...]
    og = jax.nn.sigmoid(og_ref[...].astype(F32))
    mg = mg_ref[...]
    y = _dot(bl_ref[...], wo_ref[M_WIDTH:, :])
    for h in range(M_HEADS):
        sl = slice(h * M_DV, (h + 1) * M_DV)
        xh = hm[sl]
        inv = lax.rsqrt(jnp.mean(xh * xh, axis=0, keepdims=True) + EPS)
        hn = (xh * inv * mg[sl] * og[sl]).astype(BF16)
        y = y + lax.dot_general(hn, wo_ref[sl, :], (((0,), (0,)), ((), ())),
                                preferred_element_type=F32)
    x1 = x_ref[...] + g1_ref[...] * y
    o_ref[...] = _mlp_tail(x1, g2_ref[...], sh2_ref[...], sc2_ref[...], n2g_ref[...],
                           w1_ref, w2_ref)


def _post_c_kernel(x_ref, y_ref, g1_ref, sh2_ref, sc2_ref, g2_ref, n2g_ref, fg_ref,
                   wo_ref, w1_ref, w2_ref, o_ref):
    x1 = x_ref[...] + g1_ref[...] * _dot(y_ref[...], wo_ref[...])
    x2 = _mlp_tail(x1, g2_ref[...], sh2_ref[...], sc2_ref[...], n2g_ref[...], w1_ref, w2_ref)
    o_ref[...] = _rms(x2) * fg_ref[...]


def _resident(a):
    return pl.BlockSpec(a.shape, lambda i, j: (0,) * a.ndim, pipeline_mode=pl.Buffered(1))


def _post_ab(x, h_dirs, qvo_t, b_lat, mg, mods, n2g, wo, w1, w2, tm):
    b, t, d = x.shape
    tm = min(tm, t)
    g1, sh2, sc2, g2 = mods
    bm = g1.shape[0]
    mod_map = (lambda i, j: (i, 0, 0)) if bm == b else (lambda i, j: (0, 0, 0))
    mod_spec = pl.BlockSpec((None, 1, d), mod_map)
    tok = lambda w, cb: pl.BlockSpec((None, tm, w), lambda i, j: (i, j, cb))
    return pl.pallas_call(
        _post_ab_kernel,
        grid=(b, t // tm),
        in_specs=[tok(d, 0),
                  pl.BlockSpec((None, None, M_WIDTH, tm), lambda i, j: (0, i, 0, j)),
                  pl.BlockSpec((None, None, M_WIDTH, tm), lambda i, j: (1, i, 0, j)),
                  pl.BlockSpec((None, M_WIDTH, tm), lambda i, j: (i, T_O // M_WIDTH, j)),
                  tok(B_WIDTH, 0),
                  pl.BlockSpec((M_WIDTH, 1), lambda i, j: (0, 0)),
                  mod_spec, mod_spec, mod_spec, mod_spec,
                  pl.BlockSpec((1, d), lambda i, j: (0, 0)),
                  _resident(wo), _resident(w1), _resident(w2)],
        out_specs=tok(d, 0),
        out_shape=jax.ShapeDtypeStruct((b, t, d), F32),
        compiler_params=_cparams(("parallel", "parallel")),
        name="post_ab",
    )(x, h_dirs, h_dirs, qvo_t, b_lat, mg, g1, sh2, sc2, g2, n2g.reshape(1, d), wo, w1, w2)


def _post_c(x, y, mods, n2g, fg, wo, w1, w2, tm):
    b, t, d = x.shape
    tm = min(tm, t)
    g1, sh2, sc2, g2 = mods
    mod_spec = pl.BlockSpec((None, 1, d), lambda i, j: (i, 0, 0))
    tok = pl.BlockSpec((None, tm, d), lambda i, j: (i, j, 0))
    vec = pl.BlockSpec((1, d), lambda i, j: (0, 0))
    return pl.pallas_call(
        _post_c_kernel,
        grid=(b, t // tm),
        in_specs=[tok, tok, mod_spec, mod_spec, mod_spec, mod_spec, vec, vec,
                  _resident(wo), _resident(w1), _resident(w2)],
        out_specs=tok,
        out_shape=jax.ShapeDtypeStruct((b, t, d), F32),
        compiler_params=_cparams(("parallel", "parallel")),
        name="post_c",
    )(x, y, g1, sh2, sc2, g2, n2g.reshape(1, d), fg.reshape(1, d), wo, w1, w2)


def _rope_tables(t):
    pos = np.arange(t)
    row = (pos // GRID_W).astype(np.float32)
    colp = (pos % GRID_W).astype(np.float32)
    n_f = B_ROPE // 4
    freqs = (ROPE_BASE ** (-np.arange(n_f, dtype=np.float32) / n_f)).astype(np.float32)
    ang = np.concatenate([row[:, None] * freqs, colp[:, None] * freqs], axis=-1)
    cos = np.zeros((t, LANES), np.float32)
    sin = np.zeros((t, LANES), np.float32)
    cos[:, :B_NOPE] = 1.0
    cos[:, B_NOPE:B_NOPE + 16] = np.cos(ang)
    cos[:, B_NOPE + 16:B_NOPE + 32] = np.cos(ang)
    sin[:, B_NOPE:B_NOPE + 16] = np.sin(ang)
    sin[:, B_NOPE + 16:B_NOPE + 32] = np.sin(ang)
    return tuple(jnp.asarray(a) for a in (cos, sin, cos.T.copy(), sin.T.copy()))


def _identity_tables(t):
    cos = np.zeros((t, LANES), np.float32)
    cos[:, :B_NOPE + B_ROPE] = 1.0
    sin = np.zeros((t, LANES), np.float32)
    return tuple(jnp.asarray(a) for a in (cos, sin, cos.T.copy(), sin.T.copy()))


def _layer0_weights(w_in, gate_b, w_uq, w_ukv):
    d = w_in.shape[0]
    half = B_ROPE // 2
    o_g = 4 * M_WIDTH
    o_cq = o_g + 4 * M_HEADS
    o_ckv = o_cq + B_Q_RANK
    o_kr = o_ckv + B_KV_RANK
    kr = w_in[:, o_kr:o_kr + B_ROPE]
    z = lambda n: jnp.zeros((d, n), w_in.dtype)
    kr1 = jnp.concatenate([z(B_NOPE), kr, z(LANES - B_NOPE - B_ROPE)], axis=1)
    kr2 = jnp.concatenate([z(B_NOPE), -kr[:, half:], kr[:, :half], z(LANES - B_NOPE - B_ROPE)], axis=1)
    wg = w_in[:, o_g:o_g + 4 * M_HEADS]
    gpad = z(LANES - 2 * M_HEADS)
    mq, mk, mv, mo = (w_in[:, i * M_WIDTH:(i + 1) * M_WIDTH] for i in range(4))
    w0 = jnp.concatenate([mk, w_in[:, o_cq:o_kr], kr1, kr2,
                          wg[:, :2 * M_HEADS], gpad, wg[:, 2 * M_HEADS:], gpad], axis=1).astype(BF16)
    w0_t = jnp.concatenate([mq, mv, mo], axis=1).T.astype(BF16)
    gb = jnp.zeros((2, 1, LANES), F32)
    gb = gb.at[0, 0, :2 * M_HEADS].set(gate_b[:2 * M_HEADS]).at[1, 0, :2 * M_HEADS].set(gate_b[2 * M_HEADS:])

    qr = B_Q_RANK
    wq = w_uq.reshape(qr, B_HEADS, B_NOPE + B_ROPE)
    nope, ra, rb = wq[..., :B_NOPE], wq[..., B_NOPE:B_NOPE + half], wq[..., B_NOPE + half:]
    zq = jnp.zeros((qr, B_HEADS, LANES - B_NOPE - B_ROPE), w_uq.dtype)
    wq1 = jnp.concatenate([nope, ra, rb, zq], axis=-1).reshape(qr, B_HEADS * LANES).astype(BF16)
    wq2 = jnp.concatenate([jnp.zeros_like(nope), -rb, ra, zq], axis=-1).reshape(qr, B_HEADS * LANES).astype(BF16)
    wkv = w_ukv.reshape(B_KV_RANK, B_HEADS, B_NOPE + B_V)
    wk = jnp.concatenate([wkv[..., :B_NOPE], jnp.zeros((B_KV_RANK, B_HEADS, LANES - B_NOPE), w_ukv.dtype)],
                         axis=-1).reshape(B_KV_RANK, B_HEADS * LANES).astype(BF16)
    wv = wkv[..., B_NOPE:].reshape(B_KV_RANK, B_WIDTH).astype(BF16)
    return w0, w0_t, gb, wq1.T, wq2.T, wk, wv.T


def _split_mod(mod_rows):
    return [m[:, None, :] for m in jnp.split(mod_rows, 6, axis=-1)]


def kernel(x, c, ctx, c_ctx, ada_w, ada_b, norm1_g, norm2_g, mlp_w1, mlp_w2, ab_w_in, ab_gate_b,
           ab_m_norm_g, ab_q_norm_g, ab_kv_norm_g, ab_w_uq, ab_w_ukv, ab_w_out, na_w_in,
           na_rel_bias, na_w_out, final_norm_g):
    b, t, d = x.shape
    tc = ctx.shape[1]
    c_rows = jnp.concatenate([c, c_ctx[None, :], jnp.zeros((8 - b - 1, d), F32)], axis=0)
    mod = _ada(c_rows, ada_w, ada_b)

    sh1, sc1, g1, sh2, sc2, g2 = _split_mod(mod[0, :b])
    csh1, csc1, cg1, csh2, csc2, cg2 = _split_mod(mod[0, b:b + 1])
    w0, w0_t, gb, wq1, wq2, wk, wv = _layer0_weights(ab_w_in[0], ab_gate_b[0], ab_w_uq[0],
                                                      ab_w_ukv[0])
    widths, dtypes = (P_MAIN, P_GATES), (BF16, F32)
    p_main, p_gates, qvo_t = _modproj(x, sh1, sc1, norm1_g[0], w0, widths, dtypes, tm=512, w_t=w0_t)
    pc_main, pc_gates, qvo_ct = _modproj(ctx, csh1, csc1, norm1_g[0], w0, widths, dtypes, tm=256,
                                         w_t=w0_t)

    hc_dirs, st_c, st_m = _mlstm(pc_main, pc_gates, qvo_ct, gb, None, True)
    (h_dirs,) = _mlstm(p_main, p_gates, qvo_t, gb, (st_c, st_m), False)

    qg = ab_q_norm_g[0].reshape(1, -1)
    kvg = ab_kv_norm_g[0].reshape(1, -1)
    q_l, k_l, v_l = _mlaprep(p_main, _rope_tables(t), qg, kvg, wq1, wq2, wk, wv, tm=512)
    q_c, k_c, v_c = _mlaprep(pc_main, _identity_tables(tc), qg, kvg, wq1, wq2, wk, wv, tm=256)
    b_lat = _flash(q_l, k_l, v_l, k_c, v_c, tq=2048, tk=512)
    b_ctx = _flash(q_c, k_c, v_c, None, None, tq=256, tk=tc)

    wo = ab_w_out[0].astype(BF16)
    w1 = mlp_w1[0].astype(BF16)
    w2 = mlp_w2[0].astype(BF16)
    mg = ab_m_norm_g[0].reshape(M_WIDTH, 1)
    x = _post_ab(x, h_dirs, qvo_t, b_lat, mg, (g1, sh2, sc2, g2), norm2_g[0], wo, w1, w2, tm=512)
    ctx = _post_ab(ctx, hc_dirs, qvo_ct, b_ctx, mg, (cg1, csh2, csc2, cg2), norm2_g[0],
                   wo, w1, w2, tm=256)

    sh1, sc1, g1, sh2, sc2, g2 = _split_mod(mod[1, :b])
    csh1, csc1 = _split_mod(mod[1, b:b + 1])[:2]
    cw = C_HEADS * C_HEAD_DIM
    w_q = na_w_in[0][:, :cw] * (C_HEAD_DIM ** -0.5 * LOG2E)
    w_k = na_w_in[0][:, cw:2 * cw].astype(BF16)
    w_qv_t = jnp.concatenate([w_q, na_w_in[0][:, 2 * cw:]], axis=1).T.astype(BF16)
    k_n, qv_t = _modproj(x, sh1, sc1, norm1_g[1], w_k, (cw,), (BF16,), tm=512, w_t=w_qv_t)
    kc_n, vc_t = _modproj(ctx, csh1, csc1, norm1_g[1], w_k, (cw,), (BF16,), tm=256,
                          w_t=w_qv_t[cw:])
    bias_t = _natten_bias_table(na_rel_bias[0])
    y = _natten(qv_t, k_n, kc_n, vc_t, bias_t, rb=32)
    return _post_c(x, y, (g1, sh2, sc2, g2), norm2_g[1], final_norm_g,
                   na_w_out[0].astype(BF16), mlp_w1[1].astype(BF16), mlp_w2[1].astype(BF16), tm=512)
```

```python
import functools

import numpy as np
import jax
import jax.numpy as jnp
from jax import lax
from jax.experimental import pallas as pl
from jax.experimental.pallas import tpu as pltpu

F32 = jnp.float32
BF16 = jnp.bfloat16

D_MODEL = 1024
DEPTH = 2
GRID_W = 64
EPS = 1e-6
M_HEADS = 4
M_DK = 128
M_DV = 128
B_HEADS = 8
B_Q_RANK = 256
B_KV_RANK = 128
B_NOPE = 64
B_ROPE = 32
B_V = 64
ROPE_BASE = 10000.0
C_HEADS = 16
C_HEAD_DIM = 64
NA_KH = 8
NA_KW = 16
D_FF = 4 * D_MODEL
M_WIDTH = M_HEADS * M_DV
B_WIDTH = B_HEADS * B_V

LANES = 128
NEG = -1e30
LOG2E = float(np.log2(np.e))
VMEM_LIMIT = 56 * 1024 * 1024

P_K, P_CQ, P_CKV, P_KR1, P_KR2 = 0, 512, 768, 896, 1024
P_MAIN = 1152
T_Q, T_V, T_O = 0, 512, 1024
P_GATES = 256
MLSTM_CHUNK = 256
NV = 2 * M_DV
BF16_ROWS = 16
V_EXT = B_V + BF16_ROWS


def _cparams(sem):
    return pltpu.CompilerParams(dimension_semantics=sem, vmem_limit_bytes=VMEM_LIMIT)


def _rms(x):
    return x * lax.rsqrt(jnp.mean(x * x, axis=-1, keepdims=True) + EPS)


def _dot(a, b):
    return jnp.dot(a, b, preferred_element_type=F32)


def _dot_nt(a, b):
    return lax.dot_general(a, b, (((1,), (1,)), ((), ())), preferred_element_type=F32)


def _ada_kernel(c_ref, w_ref, b_ref, o_ref):
    c = c_ref[...]
    s = c * jax.nn.sigmoid(c)
    o_ref[...] = jnp.dot(s, w_ref[...], precision=lax.Precision.HIGHEST,
                         preferred_element_type=F32) + b_ref[...]


def _ada(c_rows, ada_w, ada_b):
    depth, d, n = ada_w.shape
    tn = 1536
    return pl.pallas_call(
        _ada_kernel,
        grid=(depth, n // tn),
        in_specs=[pl.BlockSpec((8, d), lambda l, j: (0, 0)),
                  pl.BlockSpec((None, d, tn), lambda l, j: (l, 0, j)),
                  pl.BlockSpec((None, 1, tn), lambda l, j: (l, 0, j))],
        out_specs=pl.BlockSpec((None, 8, tn), lambda l, j: (l, 0, j)),
        out_shape=jax.ShapeDtypeStruct((depth, 8, n), F32),
        compiler_params=_cparams(("arbitrary", "arbitrary")),
        name="ada",
    )(c_rows, ada_w, ada_b.reshape(depth, 1, n))


def _modproj_kernel(x_ref, sh_ref, sc_ref, g_ref, w_ref, *refs, widths, has_t, chunk):
    wt_ref = refs[0] if has_t else None
    o_refs = refs[1:] if has_t else refs
    x = x_ref[...]
    h = _rms(x) * g_ref[...] * (1.0 + sc_ref[...]) + sh_ref[...]
    hb = h.astype(BF16)
    off = 0
    for o_ref, n in zip(o_refs, widths):
        for c0 in range(0, n, chunk):
            c1 = min(n, c0 + chunk)
            o_ref[:, c0:c1] = _dot(hb, w_ref[:, off + c0:off + c1]).astype(o_ref.dtype)
        off += n
    if has_t:
        ot_ref = o_refs[-1]
        for c0 in range(0, wt_ref.shape[0], chunk):
            ot_ref[c0:c0 + chunk, :] = _dot_nt(wt_ref[c0:c0 + chunk, :], hb).astype(ot_ref.dtype)


def _modproj(x, shift, scale, g, w, widths, dtypes, tm, w_t=None):
    b, t, d = x.shape
    bm = shift.shape[0]
    mod_map = (lambda i, j: (i, 0, 0)) if bm == b else (lambda i, j: (0, 0, 0))
    tm = min(tm, t)
    in_specs = [pl.BlockSpec((None, tm, d), lambda i, j: (i, j, 0)),
                pl.BlockSpec((None, 1, d), mod_map),
                pl.BlockSpec((None, 1, d), mod_map),
                pl.BlockSpec((1, d), lambda i, j: (0, 0)),
                pl.BlockSpec(w.shape, lambda i, j: (0, 0))]
    args = [x, shift, scale, g.reshape(1, d), w]
    out_specs = [pl.BlockSpec((None, tm, n), lambda i, j: (i, j, 0)) for n in widths]
    out_shape = [jax.ShapeDtypeStruct((b, t, n), dt) for n, dt in zip(widths, dtypes)]
    if w_t is not None:
        in_specs.append(pl.BlockSpec(w_t.shape, lambda i, j: (0, 0)))
        args.append(w_t)
        out_specs.append(pl.BlockSpec((None, w_t.shape[0], tm), lambda i, j: (i, 0, j)))
        out_shape.append(jax.ShapeDtypeStruct((b, w_t.shape[0], t), BF16))
    return pl.pallas_call(
        functools.partial(_modproj_kernel, widths=widths, has_t=w_t is not None, chunk=512),
        grid=(b, t // tm),
        in_specs=in_specs,
        out_specs=out_specs,
        out_shape=out_shape,
        compiler_params=_cparams(("parallel", "parallel")),
        name="modproj",
    )(*args)


def _mlstm_kernel(*refs, has_init, emit_state):
    qt_ref, k_ref, vt_ref, g_ref, gb_ref = refs[:5]
    pos = 5
    if has_init:
        c0_ref, m0_ref = refs[pos:pos + 2]
        pos += 2
    h_ref = refs[pos]
    pos += 1
    if emit_state:
        cout_ref, mout_ref = refs[pos:pos + 2]
        pos += 2
    c_s, m_s = refs[pos:pos + 2]

    d = pl.program_id(1)
    j = pl.program_id(2)
    L = k_ref.shape[0]

    @pl.when(j == 0)
    def _():
        if has_init:
            c_s[...] = c0_ref[...]
            m_s[...] = m0_ref[...]
        else:
            c_s[...] = jnp.zeros_like(c_s)
            m_s[...] = jnp.zeros_like(m_s)

    r = lax.broadcasted_iota(jnp.int32, (L, L), 0)
    c = lax.broadcasted_iota(jnp.int32, (L, L), 1)
    sgn = 1 - 2 * d
    keep = (r - c) * sgn >= 0
    keep_t = (c - r) * sgn >= 0

    gates = g_ref[...] + gb_ref[...]
    ls = jax.nn.log_sigmoid(gates)
    gates_t = gates.T[0:8]
    ls_t = jax.nn.log_sigmoid(gates_t)

    def split3(a):
        hi = a.astype(BF16)
        mid = (a - hi.astype(F32)).astype(BF16)
        lo = (a - hi.astype(F32) - mid.astype(F32)).astype(BF16)
        return hi, mid, lo

    trib = keep.astype(BF16)
    bc3 = _dot(trib, jnp.concatenate(split3(ls), axis=1))
    b_cols = bc3[:, 0:LANES] + bc3[:, LANES:2 * LANES] + bc3[:, 2 * LANES:]
    br3 = _dot_nt(jnp.concatenate(split3(ls_t), axis=0), trib)
    b_rows = br3[0:8] + br3[8:16] + br3[16:24]
    b_tot = jnp.sum(ls, axis=0, keepdims=True)

    ones_rows = jnp.ones((M_DV, L), BF16)
    scale = M_DK ** -0.5
    hsl = [slice(h * M_DK, (h + 1) * M_DK) for h in range(M_HEADS)]

    cn, m, s_raw, qc = {}, {}, {}, {}
    s_w, w_inter, m_t, vext, vw, decay, m_new = {}, {}, {}, {}, {}, {}, {}

    def phase_a(h):
        cn[h] = c_s[h]
        m[h] = m_s[h][0:1, 0:1]
        lhs = jnp.concatenate([k_ref[:, hsl[h]], cn[h].astype(BF16)], axis=0)
        sq = _dot(lhs, qt_ref[hsl[h], :])
        s_raw[h] = sq[0:L]
        qc[h] = sq[L:]

    def phase_b(h):
        col = gates[:, h:h + 1] - b_cols[:, 4 + h:5 + h]
        li_r = gates_t[h:h + 1, :]
        b_r = b_rows[4 + h:5 + h, :]
        b_last = b_tot[:, 4 + h:5 + h]
        dm = jnp.where(keep_t, b_r + col, NEG)
        inter = b_r + m[h]
        m_t[h] = jnp.maximum(jnp.max(dm, axis=0, keepdims=True), inter)
        s_w[h] = (s_raw[h] * (scale * jnp.exp(dm - m_t[h]))).astype(BF16)
        w_inter[h] = jnp.exp(inter - m_t[h]) * scale
        g_r = b_last - b_r + li_r
        m_new[h] = jnp.maximum(b_last + m[h], jnp.max(g_r, axis=1, keepdims=True))
        decay[h] = jnp.exp(b_last + m[h] - m_new[h])
        vext[h] = jnp.concatenate([vt_ref[hsl[h], :], ones_rows], axis=0)
        vw[h] = (vext[h].astype(F32) * jnp.exp(g_r - m_new[h])).astype(BF16)

    def phase_c(h):
        nd = w_inter[h] * qc[h] + _dot(vext[h], s_w[h])
        den = nd[M_DV:M_DV + 1]
        h_ref[hsl[h], :] = nd[:M_DV] / jnp.maximum(jnp.abs(den), jnp.exp(-m_t[h]))
        c_s[h] = decay[h] * cn[h] + _dot(vw[h], k_ref[:, hsl[h]])
        m_s[h] = jnp.broadcast_to(m_new[h], m_s.shape[1:])

    for phase in (phase_a, phase_b, phase_c):
        for h in range(M_HEADS):
            phase(h)

    if emit_state:
        cout_ref[...] = c_s[...]
        mout_ref[...] = m_s[...]


def _mlstm(p_main, p_gates, qvo_t, gate_b2, init, emit_state):
    b, t, _ = p_main.shape
    L = min(MLSTM_CHUNK, t)
    nc = t // L

    def cidx(d, j):
        return j + d * (nc - 1 - 2 * j)

    def rows(rb):
        return pl.BlockSpec((None, M_WIDTH, L), lambda i, d, j: (i, rb, cidx(d, j)))

    in_specs = [rows(T_Q // M_WIDTH),
                pl.BlockSpec((None, L, M_WIDTH), lambda i, d, j: (i, cidx(d, j), P_K // M_WIDTH)),
                rows(T_V // M_WIDTH),
                pl.BlockSpec((None, L, LANES), lambda i, d, j: (i, cidx(d, j), d)),
                pl.BlockSpec((None, 1, LANES), lambda i, d, j: (d, 0, 0))]
    args = [qvo_t, p_main, qvo_t, p_gates, gate_b2]
    st_c = pl.BlockSpec((None, None, M_HEADS, NV, M_DK), lambda i, d, j: (i, d, 0, 0, 0))
    st_m = pl.BlockSpec((None, None, M_HEADS, 8, LANES), lambda i, d, j: (i, d, 0, 0, 0))
    if init is not None:
        in_specs += [st_c, st_m]
        args += list(init)
    out_specs = [pl.BlockSpec((None, None, M_WIDTH, L), lambda i, d, j: (d, i, 0, cidx(d, j)))]
    out_shape = [jax.ShapeDtypeStruct((2, b, M_WIDTH, t), F32)]
    if emit_state:
        out_specs += [st_c, st_m]
        out_shape += [jax.ShapeDtypeStruct((b, 2, M_HEADS, NV, M_DK), F32),
                      jax.ShapeDtypeStruct((b, 2, M_HEADS, 8, LANES), F32)]
    return pl.pallas_call(
        functools.partial(_mlstm_kernel, has_init=init is not None, emit_state=emit_state),
        grid=(b, 2, nc),
        in_specs=in_specs,
        out_specs=out_specs,
        out_shape=out_shape,
        scratch_shapes=[pltpu.VMEM((M_HEADS, NV, M_DK), F32),
                        pltpu.VMEM((M_HEADS, 8, LANES), F32)],
        compiler_params=_cparams(("parallel", "parallel", "arbitrary")),
        name="mlstm",
    )(*args)


def _mlaprep_kernel(cq_ref, ckv_ref, kr1_ref, kr2_ref, cos_ref, sin_ref, cost_ref, sint_ref,
                    qg_ref, kvg_ref, wq1_ref, wq2_ref, wk_ref, wv_ref, qt_ref, k_ref, vt_ref):
    cos = cos_ref[...]
    sin = sin_ref[...]
    cq = cq_ref[...].astype(F32)
    cqn = (_rms(cq) * qg_ref[...]).astype(BF16)
    ckv = ckv_ref[...].astype(F32)
    ckvn = (_rms(ckv) * kvg_ref[...]).astype(BF16)
    a_scale = (B_NOPE + B_ROPE) ** -0.5 * LOG2E
    kr = kr1_ref[...].astype(F32) * cos + kr2_ref[...].astype(F32) * sin
    cos_t = cost_ref[...] * a_scale
    sin_t = sint_ref[...] * a_scale
    ones = jnp.ones((V_EXT - B_V, vt_ref.shape[1]), vt_ref.dtype)
    for c0 in range(0, B_WIDTH, LANES):
        v2 = _dot_nt(wv_ref[c0:c0 + LANES, :], ckvn).astype(vt_ref.dtype)
        for i in range(LANES // B_V):
            r0 = (c0 // B_V + i) * V_EXT
            vt_ref[r0:r0 + B_V, :] = v2[i * B_V:(i + 1) * B_V]
            vt_ref[r0 + B_V:r0 + V_EXT, :] = ones
    for h in range(B_HEADS):
        sl = slice(h * LANES, (h + 1) * LANES)
        q1 = _dot_nt(wq1_ref[sl, :], cqn)
        q2 = _dot_nt(wq2_ref[sl, :], cqn)
        qt_ref[sl, :] = (q1 * cos_t + q2 * sin_t).astype(qt_ref.dtype)
        k_ref[:, sl] = (_dot(ckvn, wk_ref[:, sl]) + kr).astype(k_ref.dtype)


def _mlaprep(p_main, tables, qg, kvg, wq1, wq2, wk, wv, tm):
    cos, sin, cos_t, sin_t = tables
    b, t, _ = p_main.shape
    tm = min(tm, t)
    hw = B_HEADS * LANES

    def tok(width, cb):
        return pl.BlockSpec((None, tm, width), lambda i, j: (i, j, cb))

    def full(a):
        return pl.BlockSpec(a.shape, lambda i, j: (0,) * a.ndim)

    return pl.pallas_call(
        _mlaprep_kernel,
        grid=(b, t // tm),
        in_specs=[tok(B_Q_RANK, P_CQ // B_Q_RANK), tok(LANES, P_CKV // LANES),
                  tok(LANES, P_KR1 // LANES), tok(LANES, P_KR2 // LANES),
                  pl.BlockSpec((tm, LANES), lambda i, j: (j, 0)),
                  pl.BlockSpec((tm, LANES), lambda i, j: (j, 0)),
                  pl.BlockSpec((LANES, tm), lambda i, j: (0, j)),
                  pl.BlockSpec((LANES, tm), lambda i, j: (0, j)),
                  full(qg), full(kvg), full(wq1), full(wq2), full(wk), full(wv)],
        out_specs=[pl.BlockSpec((None, hw, tm), lambda i, j: (i, 0, j)),
                   pl.BlockSpec((None, tm, hw), lambda i, j: (i, j, 0)),
                   pl.BlockSpec((None, B_HEADS * V_EXT, tm), lambda i, j: (i, 0, j))],
        out_shape=[jax.ShapeDtypeStruct((b, hw, t), BF16),
                   jax.ShapeDtypeStruct((b, t, hw), BF16),
                   jax.ShapeDtypeStruct((b, B_HEADS * V_EXT, t), BF16)],
        compiler_params=_cparams(("parallel", "parallel")),
        name="mlaprep",
    )(p_main, p_main, p_main, p_main, cos, sin, cos_t, sin_t, qg, kvg, wq1, wq2, wk, wv)


def _flash_kernel(*refs, tk, nk, has_ctx):
    if has_ctx:
        q_ref, k_ref, v_ref, kc_ref, vc_ref, o_ref, s_s, p_s = refs
    else:
        q_ref, k_ref, v_ref, o_ref, s_s, p_s = refs
    tq = q_ref.shape[1]
    sls = [slice(h * LANES, (h + 1) * LANES) for h in range(2)]
    vsl = [slice(h * V_EXT, (h + 1) * V_EXT) for h in range(2)]

    def stage_a(h, slot, kb):
        s = _dot(kb, q_ref[sls[h], :])
        s_s[h, slot, 0:kb.shape[0]] = s
        return jnp.max(s, axis=0, keepdims=True)

    def stage_b(h, slot, rows, mx, m):
        m_new = jnp.maximum(m, mx)
        p_s[h, slot, 0:rows] = jnp.exp2(s_s[h, slot, 0:rows] - m_new).astype(BF16)
        return m_new, jnp.exp2(m - m_new)

    def stage_c(h, slot, vb, alpha, acc):
        return alpha * acc + _dot(vb, p_s[h, slot, 0:vb.shape[1]])

    def kblock(j, h):
        return k_ref[pl.ds(pl.multiple_of(j * tk, tk), tk), sls[h]]

    def vblock(j, h):
        return v_ref[vsl[h], pl.ds(pl.multiple_of(j * tk, LANES), tk)]

    def trip(slot, state, k_new, v_old):
        new = []
        for h in range(2):
            mx, m, alpha, acc = state[h]
            mx_new = stage_a(h, slot, k_new(h))
            m, alpha_new = stage_b(h, 1 - slot, tk, mx, m)
            acc = stage_c(h, slot, v_old(h), alpha, acc)
            new.append((mx_new, m, alpha_new, acc))
        return tuple(new)

    zrow = jnp.zeros((1, tq), F32)
    neg = jnp.full((1, tq), NEG, F32)
    zacc = jnp.zeros((V_EXT, tq), F32)
    state = []
    for h in range(2):
        mx = stage_a(h, 0, k_ref[0:tk, sls[h]])
        if nk > 1:
            mx1 = stage_a(h, 1, k_ref[tk:2 * tk, sls[h]])
            m, alpha = stage_b(h, 0, tk, mx, neg)
            state.append((mx1, m, alpha, zacc))
        else:
            state.append((mx, neg, zrow, zacc))
    state = tuple(state)
    if nk > 2:
        assert nk % 2 == 0

        def body(i, state):
            for par in range(2):
                j = 2 * i + par
                state = trip(par, state, functools.partial(kblock, j),
                             functools.partial(vblock, j - 2))
            return state

        state = lax.fori_loop(1, nk // 2, body, state)
    last = (nk - 1) % 2
    tail_rows = tk
    if has_ctx:
        assert nk > 1 and kc_ref.shape[0] <= tk
        state = trip(1 - last, state, lambda h: kc_ref[:, sls[h]],
                     lambda h: v_ref[vsl[h], (nk - 2) * tk:(nk - 1) * tk])
        tail_rows = kc_ref.shape[0]
    outs = []
    for h in range(2):
        mx, m, alpha, acc = state[h]
        if has_ctx:
            acc = stage_c(h, last, v_ref[vsl[h], (nk - 1) * tk:nk * tk], alpha, acc)
            m, alpha = stage_b(h, 1 - last, tail_rows, mx, m)
            acc = stage_c(h, 1 - last, vc_ref[vsl[h], :], alpha, acc)
        else:
            if nk > 1:
                acc = stage_c(h, 1 - last, v_ref[vsl[h], (nk - 2) * tk:(nk - 1) * tk], alpha, acc)
            m, alpha = stage_b(h, last, tail_rows, mx, m)
            acc = stage_c(h, last, v_ref[vsl[h], (nk - 1) * tk:nk * tk], alpha, acc)
        outs.append(acc[0:B_V] / acc[B_V:B_V + 1])
    o_ref[...] = jnp.concatenate(outs, axis=0).T.astype(o_ref.dtype)


def _flash(q_t, k, v_t, kc, vc_t, tq, tk):
    b, _, t = q_t.shape
    tkeys = k.shape[1]
    tq = min(tq, t)
    tk = min(tk, tkeys)
    assert tkeys % tk == 0
    has_ctx = kc is not None
    in_specs = [pl.BlockSpec((None, 2 * LANES, tq), lambda i, hp, j: (i, hp, j)),
                pl.BlockSpec((None, tkeys, 2 * LANES), lambda i, hp, j: (i, 0, hp)),
                pl.BlockSpec((None, 2 * V_EXT, tkeys), lambda i, hp, j: (i, hp, 0))]
    args = [q_t, k, v_t]
    if has_ctx:
        tc = kc.shape[1]
        in_specs += [pl.BlockSpec((None, tc, 2 * LANES), lambda i, hp, j: (i, 0, hp)),
                     pl.BlockSpec((None, 2 * V_EXT, tc), lambda i, hp, j: (i, hp, 0))]
        args += [kc, vc_t]
    return pl.pallas_call(
        functools.partial(_flash_kernel, tk=tk, nk=tkeys // tk, has_ctx=has_ctx),
        grid=(b, B_HEADS // 2, t // tq),
        in_specs=in_specs,
        out_specs=pl.BlockSpec((None, tq, LANES), lambda i, hp, j: (i, j, hp)),
        out_shape=jax.ShapeDtypeStruct((b, t, B_WIDTH), BF16),
        scratch_shapes=[pltpu.VMEM((2, 2, tk, tq), F32), pltpu.VMEM((2, 2, tk, tq), BF16)],
        compiler_params=_cparams(("parallel", "parallel", "arbitrary")),
        name="flash",
    )(*args)


NA_PAIR_ROWS = NA_KH + 2
NA_PAIR_KEYS = NA_PAIR_ROWS * GRID_W
NA_VARIANTS = 5


def _natten_pair_window(r, rows):
    return jnp.clip(r - NA_KH // 2, 0, rows - NA_PAIR_ROWS)


def _natten_kernel(qt_ref, k_ref, vt_ref, kc_ref, vct_ref, bias_ref, o_ref, *, rb, rows):
    blk = pl.program_id(2)
    tq = rb * GRID_W
    pq = 2 * GRID_W
    npair = rb // 2
    q_t = qt_ref[...]
    feat = lax.broadcasted_iota(jnp.int32, q_t.shape, 0)
    zero = jnp.zeros_like(q_t)
    qh = [jnp.where(feat < C_HEAD_DIM, q_t, zero), jnp.where(feat >= C_HEAD_DIM, q_t, zero)]
    q2 = [jnp.concatenate([qh[0][:, i * pq:(i + 1) * pq], qh[1][:, i * pq:(i + 1) * pq]], axis=1)
          for i in range(npair)]
    s_c = _dot(kc_ref[...], jnp.concatenate(q2, axis=1))
    m_c = jnp.max(s_c, axis=0, keepdims=True)

    def scores(i):
        r = blk * rb + 2 * i
        ws = _natten_pair_window(r, rows)
        koff = pl.multiple_of(ws * GRID_W, LANES)
        s_w = _dot(k_ref[pl.ds(koff, NA_PAIR_KEYS), :], q2[i])
        var = (r - ws) // 2
        return koff, s_w + jnp.concatenate([bias_ref[0, var], bias_ref[1, var]], axis=1)

    def softmax(i, s_w):
        csl = slice(i * 2 * pq, (i + 1) * 2 * pq)
        m = jnp.maximum(jnp.max(s_w, axis=0, keepdims=True), m_c[:, csl])
        p_w = jnp.exp2(s_w - m)
        p_ci = jnp.exp2(s_c[:, csl] - m)
        l = jnp.sum(p_w, axis=0, keepdims=True) + jnp.sum(p_ci, axis=0, keepdims=True)
        return p_w.astype(BF16), p_ci.astype(BF16), l

    o_w, p_c, l_all = [], [], []
    sc = {0: scores(0)}
    if npair > 1:
        sc[1] = scores(1)
    sm = {0: softmax(0, sc[0][1])}
    for i in range(npair):
        if i + 2 < npair:
            sc[i + 2] = scores(i + 2)
        if i + 1 < npair:
            sm[i + 1] = softmax(i + 1, sc[i + 1][1])
        p_w, p_ci, l = sm.pop(i)
        koff = sc.pop(i)[0]
        p_c.append(p_ci)
        l_all.append(l)
        o_w.append(_dot(vt_ref[:, pl.ds(koff, NA_PAIR_KEYS)], p_w))
    o2 = ((jnp.concatenate(o_w, axis=1) + _dot(vct_ref[...], jnp.concatenate(p_c, axis=1)))
          / jnp.concatenate(l_all, axis=1))
    top = lax.broadcasted_iota(jnp.int32, (LANES, pq), 0) < C_HEAD_DIM
    out_t = jnp.concatenate(
        [jnp.where(top, o2[:, i * 2 * pq:i * 2 * pq + pq], o2[:, i * 2 * pq + pq:(i + 1) * 2 * pq])
         for i in range(npair)], axis=1)
    o_ref[...] = out_t.T.astype(o_ref.dtype)


def _natten(qv_t, k, kc, vc_t, bias_t, rb):
    b, t, _ = k.shape
    tc = kc.shape[1]
    rows = t // GRID_W
    rb = min(rb, rows)
    assert rows % 2 == 0 and rows >= NA_PAIR_ROWS + 4 and rb % 2 == 0
    nhp = C_HEADS // 2
    return pl.pallas_call(
        functools.partial(_natten_kernel, rb=rb, rows=rows),
        grid=(b, nhp, rows // rb),
        in_specs=[pl.BlockSpec((None, LANES, rb * GRID_W), lambda i, hp, j: (i, hp, j)),
                  pl.BlockSpec((None, t, LANES), lambda i, hp, j: (i, 0, hp)),
                  pl.BlockSpec((None, LANES, t), lambda i, hp, j: (i, nhp + hp, 0)),
                  pl.BlockSpec((None, tc, LANES), lambda i, hp, j: (i, 0, hp)),
                  pl.BlockSpec((None, LANES, tc), lambda i, hp, j: (i, hp, 0)),
                  pl.BlockSpec((None,) + bias_t.shape[1:], lambda i, hp, j: (hp, 0, 0, 0, 0))],
        out_specs=pl.BlockSpec((None, rb * GRID_W, LANES), lambda i, hp, j: (i, j, hp)),
        out_shape=jax.ShapeDtypeStruct((b, t, C_HEADS * C_HEAD_DIM), BF16),
        compiler_params=_cparams(("parallel", "parallel", "arbitrary")),
        name="natten",
    )(qv_t, k, qv_t, kc, vc_t, bias_t)


def _natten_bias_table(rel_bias):
    kcol = np.arange(GRID_W)[:, None]
    qc = np.arange(GRID_W)[None, :]
    cs = np.clip(qc - NA_KW // 2, 0, GRID_W - NA_KW)
    valid = (kcol >= cs) & (kcol < cs + NA_KW)
    pad = GRID_W - NA_KW
    rev = jnp.pad(rel_bias.astype(F32) * LOG2E, ((0, 0), (0, 0), (pad, pad)))[:, :, ::-1]
    toe = jnp.stack([rev[:, :, GRID_W - 1 - k:2 * GRID_W - 1 - k] for k in range(GRID_W)], axis=2)
    toe = jnp.where(valid[None, None], toe, NEG)
    nri = 2 * NA_KH - 1
    neg = jnp.full_like(toe[:, :1], NEG)
    up = jnp.concatenate([toe, neg], axis=1)
    down = jnp.concatenate([neg, toe], axis=1)
    negs = jnp.full_like(up, NEG)
    both = jnp.concatenate([up, down], axis=-1)
    left = jnp.concatenate([up, negs], axis=-1)
    right = jnp.concatenate([negs, down], axis=-1)
    none = jnp.full_like(both[:, 0], NEG)
    own = ((0, 0), (0, 0), (0, 1), (2, 2), (2, 2))
    variants = []
    for v in range(NA_VARIANTS):
        key_rows = []
        for x in range(NA_PAIR_ROWS):
            ri0 = x - 2 * v + NA_KH - 1
            has = [own[v][j] <= x < own[v][j] + NA_KH for j in range(2)]
            assert not any(has) or 0 <= ri0 <= nri
            src = both if all(has) else left if has[0] else right if has[1] else None
            key_rows.append(none if src is None else src[:, ri0])
        variants.append(jnp.concatenate(key_rows, axis=1))
    tab = jnp.stack(variants, axis=1)
    return tab.reshape(C_HEADS // 2, 2, NA_VARIANTS, NA_PAIR_KEYS, 2 * GRID_W)


def _mlp_tail(x1, g2, sh2, sc2, n2g, w1_ref, w2_ref):
    h = (_rms(x1) * n2g * (1.0 + sc2) + sh2).astype(BF16)
    acc = jnp.zeros_like(x1)
    chunk = 1024
    for c0 in range(0, D_FF, chunk):
        a = jnp.maximum(_dot(h, w1_ref[:, c0:c0 + chunk]), 0.0)
        acc = acc + _dot((a * a).astype(BF16), w2_ref[c0:c0 + chunk, :])
    return x1 + g2 * acc


def _post_ab_kernel(x_ref, hf_ref, hb_ref, og_ref, bl_ref, mg_ref, g1_ref, sh2_ref, sc2_ref,
                    g2_ref, n2g_ref, wo_ref, w1_ref, w2_ref, o_ref):
    hm = hf_ref[...] + hb_ref[...]
    og = jax.nn.sigmoid(og_ref[...].astype(F32))
    mg = mg_ref[...]
    y = _dot(bl_ref[...], wo_ref[M_WIDTH:, :])
    for h in range(M_HEADS):
        sl = slice(h * M_DV, (h + 1) * M_DV)
        xh = hm[sl]
        inv = lax.rsqrt(jnp.mean(xh * xh, axis=0, keepdims=True) + EPS)
        hn = (xh * inv * mg[sl] * og[sl]).astype(BF16)
        y = y + lax.dot_general(hn, wo_ref[sl, :], (((0,), (0,)), ((), ())),
                                preferred_element_type=F32)
    x1 = x_ref[...] + g1_ref[...] * y
    o_ref[...] = _mlp_tail(x1, g2_ref[...], sh2_ref[...], sc2_ref[...], n2g_ref[...],
                           w1_ref, w2_ref)


def _post_c_kernel(x_ref, y_ref, g1_ref, sh2_ref, sc2_ref, g2_ref, n2g_ref, fg_ref,
                   wo_ref, w1_ref, w2_ref, o_ref):
    x1 = x_ref[...] + g1_ref[...] * _dot(y_ref[...], wo_ref[...])
    x2 = _mlp_tail(x1, g2_ref[...], sh2_ref[...], sc2_ref[...], n2g_ref[...], w1_ref, w2_ref)
    o_ref[...] = _rms(x2) * fg_ref[...]


def _resident(a):
    return pl.BlockSpec(a.shape, lambda i, j: (0,) * a.ndim, pipeline_mode=pl.Buffered(1))


def _post_ab(x, h_dirs, qvo_t, b_lat, mg, mods, n2g, wo, w1, w2, tm):
    b, t, d = x.shape
    tm = min(tm, t)
    g1, sh2, sc2, g2 = mods
    bm = g1.shape[0]
    mod_map = (lambda i, j: (i, 0, 0)) if bm == b else (lambda i, j: (0, 0, 0))
    mod_spec = pl.BlockSpec((None, 1, d), mod_map)
    tok = lambda w, cb: pl.BlockSpec((None, tm, w), lambda i, j: (i, j, cb))
    return pl.pallas_call(
        _post_ab_kernel,
        grid=(b, t // tm),
        in_specs=[tok(d, 0),
                  pl.BlockSpec((None, None, M_WIDTH, tm), lambda i, j: (0, i, 0, j)),
                  pl.BlockSpec((None, None, M_WIDTH, tm), lambda i, j: (1, i, 0, j)),
                  pl.BlockSpec((None, M_WIDTH, tm), lambda i, j: (i, T_O // M_WIDTH, j)),
                  tok(B_WIDTH, 0),
                  pl.BlockSpec((M_WIDTH, 1), lambda i, j: (0, 0)),
                  mod_spec, mod_spec, mod_spec, mod_spec,
                  pl.BlockSpec((1, d), lambda i, j: (0, 0)),
                  _resident(wo), _resident(w1), _resident(w2)],
        out_specs=tok(d, 0),
        out_shape=jax.ShapeDtypeStruct((b, t, d), F32),
        compiler_params=_cparams(("parallel", "parallel")),
        name="post_ab",
    )(x, h_dirs, h_dirs, qvo_t, b_lat, mg, g1, sh2, sc2, g2, n2g.reshape(1, d), wo, w1, w2)


def _post_c(x, y, mods, n2g, fg, wo, w1, w2, tm):
    b, t, d = x.shape
    tm = min(tm, t)
    g1, sh2, sc2, g2 = mods
    mod_spec = pl.BlockSpec((None, 1, d), lambda i, j: (i, 0, 0))
    tok = pl.BlockSpec((None, tm, d), lambda i, j: (i, j, 0))
    vec = pl.BlockSpec((1, d), lambda i, j: (0, 0))
    return pl.pallas_call(
        _post_c_kernel,
        grid=(b, t // tm),
        in_specs=[tok, tok, mod_spec, mod_spec, mod_spec, mod_spec, vec, vec,
                  _resident(wo), _resident(w1), _resident(w2)],
        out_specs=tok,
        out_shape=jax.ShapeDtypeStruct((b, t, d), F32),
        compiler_params=_cparams(("parallel", "parallel")),
        name="post_c",
    )(x, y, g1, sh2, sc2, g2, n2g.reshape(1, d), fg.reshape(1, d), wo, w1, w2)


def _rope_tables(t):
    pos = np.arange(t)
    row = (pos // GRID_W).astype(np.float32)
    colp = (pos % GRID_W).astype(np.float32)
    n_f = B_ROPE // 4
    freqs = (ROPE_BASE ** (-np.arange(n_f, dtype=np.float32) / n_f)).astype(np.float32)
    ang = np.concatenate([row[:, None] * freqs, colp[:, None] * freqs], axis=-1)
    cos = np.zeros((t, LANES), np.float32)
    sin = np.zeros((t, LANES), np.float32)
    cos[:, :B_NOPE] = 1.0
    cos[:, B_NOPE:B_NOPE + 16] = np.cos(ang)
    cos[:, B_NOPE + 16:B_NOPE + 32] = np.cos(ang)
    sin[:, B_NOPE:B_NOPE + 16] = np.sin(ang)
    sin[:, B_NOPE + 16:B_NOPE + 32] = np.sin(ang)
    return tuple(jnp.asarray(a) for a in (cos, sin, cos.T.copy(), sin.T.copy()))


def _identity_tables(t):
    cos = np.zeros((t, LANES), np.float32)
    cos[:, :B_NOPE + B_ROPE] = 1.0
    sin = np.zeros((t, LANES), np.float32)
    return tuple(jnp.asarray(a) for a in (cos, sin, cos.T.copy(), sin.T.copy()))


def _layer0_weights(w_in, gate_b, w_uq, w_ukv):
    d = w_in.shape[0]
    half = B_ROPE // 2
    o_g = 4 * M_WIDTH
    o_cq = o_g + 4 * M_HEADS
    o_ckv = o_cq + B_Q_RANK
    o_kr = o_ckv + B_KV_RANK
    kr = w_in[:, o_kr:o_kr + B_ROPE]
    z = lambda n: jnp.zeros((d, n), w_in.dtype)
    kr1 = jnp.concatenate([z(B_NOPE), kr, z(LANES - B_NOPE - B_ROPE)], axis=1)
    kr2 = jnp.concatenate([z(B_NOPE), -kr[:, half:], kr[:, :half], z(LANES - B_NOPE - B_ROPE)], axis=1)
    wg = w_in[:, o_g:o_g + 4 * M_HEADS]
    gpad = z(LANES - 2 * M_HEADS)
    mq, mk, mv, mo = (w_in[:, i * M_WIDTH:(i + 1) * M_WIDTH] for i in range(4))
    w0 = jnp.concatenate([mk, w_in[:, o_cq:o_kr], kr1, kr2,
                          wg[:, :2 * M_HEADS], gpad, wg[:, 2 * M_HEADS:], gpad], axis=1).astype(BF16)
    w0_t = jnp.concatenate([mq, mv, mo], axis=1).T.astype(BF16)
    gb = jnp.zeros((2, 1, LANES), F32)
    gb = gb.at[0, 0, :2 * M_HEADS].set(gate_b[:2 * M_HEADS]).at[1, 0, :2 * M_HEADS].set(gate_b[2 * M_HEADS:])

    qr = B_Q_RANK
    wq = w_uq.reshape(qr, B_HEADS, B_NOPE + B_ROPE)
    nope, ra, rb = wq[..., :B_NOPE], wq[..., B_NOPE:B_NOPE + half], wq[..., B_NOPE + half:]
    zq = jnp.zeros((qr, B_HEADS, LANES - B_NOPE - B_ROPE), w_uq.dtype)
    wq1 = jnp.concatenate([nope, ra, rb, zq], axis=-1).reshape(qr, B_HEADS * LANES).astype(BF16)
    wq2 = jnp.concatenate([jnp.zeros_like(nope), -rb, ra, zq], axis=-1).reshape(qr, B_HEADS * LANES).astype(BF16)
    wkv = w_ukv.reshape(B_KV_RANK, B_HEADS, B_NOPE + B_V)
    wk = jnp.concatenate([wkv[..., :B_NOPE], jnp.zeros((B_KV_RANK, B_HEADS, LANES - B_NOPE), w_ukv.dtype)],
                         axis=-1).reshape(B_KV_RANK, B_HEADS * LANES).astype(BF16)
    wv = wkv[..., B_NOPE:].reshape(B_KV_RANK, B_WIDTH).astype(BF16)
    return w0, w0_t, gb, wq1.T, wq2.T, wk, wv.T


def _split_mod(mod_rows):
    return [m[:, None, :] for m in jnp.split(mod_rows, 6, axis=-1)]


def kernel(x, c, ctx, c_ctx, ada_w, ada_b, norm1_g, norm2_g, mlp_w1, mlp_w2, ab_w_in, ab_gate_b,
           ab_m_norm_g, ab_q_norm_g, ab_kv_norm_g, ab_w_uq, ab_w_ukv, ab_w_out, na_w_in,
           na_rel_bias, na_w_out, final_norm_g):
    b, t, d = x.shape
    tc = ctx.shape[1]
    c_rows = jnp.concatenate([c, c_ctx[None, :], jnp.zeros((8 - b - 1, d), F32)], axis=0)
    mod = _ada(c_rows, ada_w, ada_b)

    sh1, sc1, g1, sh2, sc2, g2 = _split_mod(mod[0, :b])
    csh1, csc1, cg1, csh2, csc2, cg2 = _split_mod(mod[0, b:b + 1])
    w0, w0_t, gb, wq1, wq2, wk, wv = _layer0_weights(ab_w_in[0], ab_gate_b[0], ab_w_uq[0],
                                                      ab_w_ukv[0])
    widths, dtypes = (P_MAIN, P_GATES), (BF16, F32)
    p_main, p_gates, qvo_t = _modproj(x, sh1, sc1, norm1_g[0], w0, widths, dtypes, tm=512, w_t=w0_t)
    pc_main, pc_gates, qvo_ct = _modproj(ctx, csh1, csc1, norm1_g[0], w0, widths, dtypes, tm=256,
                                         w_t=w0_t)

    hc_dirs, st_c, st_m = _mlstm(pc_main, pc_gates, qvo_ct, gb, None, True)
    (h_dirs,) = _mlstm(p_main, p_gates, qvo_t, gb, (st_c, st_m), False)

    qg = ab_q_norm_g[0].reshape(1, -1)
    kvg = ab_kv_norm_g[0].reshape(1, -1)
    q_l, k_l, v_l = _mlaprep(p_main, _rope_tables(t), qg, kvg, wq1, wq2, wk, wv, tm=512)
    q_c, k_c, v_c = _mlaprep(pc_main, _identity_tables(tc), qg, kvg, wq1, wq2, wk, wv, tm=256)
    b_lat = _flash(q_l, k_l, v_l, k_c, v_c, tq=2048, tk=512)
    b_ctx = _flash(q_c, k_c, v_c, None, None, tq=256, tk=tc)

    wo = ab_w_out[0].astype(BF16)
    w1 = mlp_w1[0].astype(BF16)
    w2 = mlp_w2[0].astype(BF16)
    mg = ab_m_norm_g[0].reshape(M_WIDTH, 1)
    x = _post_ab(x, h_dirs, qvo_t, b_lat, mg, (g1, sh2, sc2, g2), norm2_g[0], wo, w1, w2, tm=512)
    ctx = _post_ab(ctx, hc_dirs, qvo_ct, b_ctx, mg, (cg1, csh2, csc2, cg2), norm2_g[0],
                   wo, w1, w2, tm=256)

    sh1, sc1, g1, sh2, sc2, g2 = _split_mod(mod[1, :b])
    csh1, csc1 = _split_mod(mod[1, b:b + 1])[:2]
    cw = C_HEADS * C_HEAD_DIM
    w_q = na_w_in[0][:, :cw] * (C_HEAD_DIM ** -0.5 * LOG2E)
    w_k = na_w_in[0][:, cw:2 * cw].astype(BF16)
    w_qv_t = jnp.concatenate([w_q, na_w_in[0][:, 2 * cw:]], axis=1).T.astype(BF16)
    k_n, qv_t = _modproj(x, sh1, sc1, norm1_g[1], w_k, (cw,), (BF16,), tm=512, w_t=w_qv_t)
    kc_n, vc_t = _modproj(ctx, csh1, csc1, norm1_g[1], w_k, (cw,), (BF16,), tm=256,
                          w_t=w_qv_t[cw:])
    bias_t = _natten_bias_table(na_rel_bias[0])
    y = _natten(qv_t, k_n, kc_n, vc_t, bias_t, rb=64)
    return _post_c(x, y, (g1, sh2, sc2, g2), norm2_g[1], final_norm_g,
                   na_w_out[0].astype(BF16), mlp_w1[1].astype(BF16), mlp_w2[1].astype(BF16), tm=512)
```

```python
import functools

import numpy as np
import jax
import jax.numpy as jnp
from jax import lax
from jax.experimental import pallas as pl
from jax.experimental.pallas import tpu as pltpu

F32 = jnp.float32
BF16 = jnp.bfloat16

D_MODEL = 1024
DEPTH = 2
GRID_W = 64
EPS = 1e-6
M_HEADS = 4
M_DK = 128
M_DV = 128
B_HEADS = 8
B_Q_RANK = 256
B_KV_RANK = 128
B_NOPE = 64
B_ROPE = 32
B_V = 64
ROPE_BASE = 10000.0
C_HEADS = 16
C_HEAD_DIM = 64
NA_KH = 8
NA_KW = 16
D_FF = 4 * D_MODEL
M_WIDTH = M_HEADS * M_DV
B_WIDTH = B_HEADS * B_V

LANES = 128
NEG = -1e30
LOG2E = float(np.log2(np.e))
VMEM_LIMIT = 56 * 1024 * 1024

P_K, P_CQ, P_CKV, P_KR1, P_KR2 = 0, 512, 768, 896, 1024
P_MAIN = 1152
T_Q, T_V, T_O = 0, 512, 1024
P_GATES = 256
MLSTM_CHUNK = 256
NV = 2 * M_DV
BF16_ROWS = 16
V_EXT = B_V + BF16_ROWS


def _cparams(sem):
    return pltpu.CompilerParams(dimension_semantics=sem, vmem_limit_bytes=VMEM_LIMIT)


def _rms(x):
    return x * lax.rsqrt(jnp.mean(x * x, axis=-1, keepdims=True) + EPS)


def _dot(a, b):
    return jnp.dot(a, b, preferred_element_type=F32)


def _dot_nt(a, b):
    return lax.dot_general(a, b, (((1,), (1,)), ((), ())), preferred_element_type=F32)


def _ada_kernel(c_ref, w_ref, b_ref, o_ref):
    c = c_ref[...]
    s = c * jax.nn.sigmoid(c)
    o_ref[...] = jnp.dot(s, w_ref[...], precision=lax.Precision.HIGHEST,
                         preferred_element_type=F32) + b_ref[...]


def _ada(c_rows, ada_w, ada_b):
    depth, d, n = ada_w.shape
    tn = 1536
    return pl.pallas_call(
        _ada_kernel,
        grid=(depth, n // tn),
        in_specs=[pl.BlockSpec((8, d), lambda l, j: (0, 0)),
                  pl.BlockSpec((None, d, tn), lambda l, j: (l, 0, j)),
                  pl.BlockSpec((None, 1, tn), lambda l, j: (l, 0, j))],
        out_specs=pl.BlockSpec((None, 8, tn), lambda l, j: (l, 0, j)),
        out_shape=jax.ShapeDtypeStruct((depth, 8, n), F32),
        compiler_params=_cparams(("arbitrary", "arbitrary")),
        name="ada",
    )(c_rows, ada_w, ada_b.reshape(depth, 1, n))


def _modproj_kernel(x_ref, sh_ref, sc_ref, g_ref, w_ref, *refs, widths, has_t, chunk):
    wt_ref = refs[0] if has_t else None
    o_refs = refs[1:] if has_t else refs
    x = x_ref[...]
    h = _rms(x) * g_ref[...] * (1.0 + sc_ref[...]) + sh_ref[...]
    hb = h.astype(BF16)
    off = 0
    for o_ref, n in zip(o_refs, widths):
        for c0 in range(0, n, chunk):
            c1 = min(n, c0 + chunk)
            o_ref[:, c0:c1] = _dot(hb, w_ref[:, off + c0:off + c1]).astype(o_ref.dtype)
        off += n
    if has_t:
        ot_ref = o_refs[-1]
        for c0 in range(0, wt_ref.shape[0], chunk):
            ot_ref[c0:c0 + chunk, :] = _dot_nt(wt_ref[c0:c0 + chunk, :], hb).astype(ot_ref.dtype)


def _modproj(x, shift, scale, g, w, widths, dtypes, tm, w_t=None):
    b, t, d = x.shape
    bm = shift.shape[0]
    mod_map = (lambda i, j: (i, 0, 0)) if bm == b else (lambda i, j: (0, 0, 0))
    tm = min(tm, t)
    in_specs = [pl.BlockSpec((None, tm, d), lambda i, j: (i, j, 0)),
                pl.BlockSpec((None, 1, d), mod_map),
                pl.BlockSpec((None, 1, d), mod_map),
                pl.BlockSpec((1, d), lambda i, j: (0, 0)),
                pl.BlockSpec(w.shape, lambda i, j: (0, 0))]
    args = [x, shift, scale, g.reshape(1, d), w]
    out_specs = [pl.BlockSpec((None, tm, n), lambda i, j: (i, j, 0)) for n in widths]
    out_shape = [jax.ShapeDtypeStruct((b, t, n), dt) for n, dt in zip(widths, dtypes)]
    if w_t is not None:
        in_specs.append(pl.BlockSpec(w_t.shape, lambda i, j: (0, 0)))
        args.append(w_t)
        out_specs.append(pl.BlockSpec((None, w_t.shape[0], tm), lambda i, j: (i, 0, j)))
        out_shape.append(jax.ShapeDtypeStruct((b, w_t.shape[0], t), BF16))
    return pl.pallas_call(
        functools.partial(_modproj_kernel, widths=widths, has_t=w_t is not None, chunk=512),
        grid=(b, t // tm),
        in_specs=in_specs,
        out_specs=out_specs,
        out_shape=out_shape,
        compiler_params=_cparams(("parallel", "parallel")),
        name="modproj",
    )(*args)


def _mlstm_direction(d, qt_ref, k_ref, vt_ref, g_ref, gb, h_ref, c_s, m_s):
    L = k_ref.shape[0]
    r = lax.broadcasted_iota(jnp.int32, (L, L), 0)
    c = lax.broadcasted_iota(jnp.int32, (L, L), 1)
    keep = (r >= c) if d == 0 else (r <= c)
    keep_t = (c >= r) if d == 0 else (c <= r)

    gates = g_ref[...] + gb
    ls = jax.nn.log_sigmoid(gates)
    gates_t = gates.T[0:8]
    ls_t = jax.nn.log_sigmoid(gates_t)

    def split3(a):
        hi = a.astype(BF16)
        mid = (a - hi.astype(F32)).astype(BF16)
        lo = (a - hi.astype(F32) - mid.astype(F32)).astype(BF16)
        return hi, mid, lo

    trib = keep.astype(BF16)
    bc3 = _dot(trib, jnp.concatenate(split3(ls), axis=1))
    b_cols = bc3[:, 0:LANES] + bc3[:, LANES:2 * LANES] + bc3[:, 2 * LANES:]
    br3 = _dot_nt(jnp.concatenate(split3(ls_t), axis=0), trib)
    b_rows = br3[0:8] + br3[8:16] + br3[16:24]
    b_tot = jnp.sum(ls, axis=0, keepdims=True)

    ones_rows = jnp.ones((M_DV, L), BF16)
    scale = M_DK ** -0.5
    hsl = [slice(h * M_DK, (h + 1) * M_DK) for h in range(M_HEADS)]

    cn, m, s_raw, qc = {}, {}, {}, {}
    s_w, w_inter, m_t, vext, vw, decay, m_new = {}, {}, {}, {}, {}, {}, {}

    def phase_a(h):
        cn[h] = c_s[d, h]
        m[h] = m_s[d, h][0:1, 0:1]
        lhs = jnp.concatenate([k_ref[:, hsl[h]], cn[h].astype(BF16)], axis=0)
        sq = _dot(lhs, qt_ref[hsl[h], :])
        s_raw[h] = sq[0:L]
        qc[h] = sq[L:]

    def phase_b(h):
        col = gates[:, h:h + 1] - b_cols[:, 4 + h:5 + h]
        li_r = gates_t[h:h + 1, :]
        b_r = b_rows[4 + h:5 + h, :]
        b_last = b_tot[:, 4 + h:5 + h]
        dm = jnp.where(keep_t, b_r + col, NEG)
        inter = b_r + m[h]
        m_t[h] = jnp.maximum(jnp.max(dm, axis=0, keepdims=True), inter)
        s_w[h] = (s_raw[h] * (scale * jnp.exp(dm - m_t[h]))).astype(BF16)
        w_inter[h] = jnp.exp(inter - m_t[h]) * scale
        g_r = b_last - b_r + li_r
        m_new[h] = jnp.maximum(b_last + m[h], jnp.max(g_r, axis=1, keepdims=True))
        decay[h] = jnp.exp(b_last + m[h] - m_new[h])
        vext[h] = jnp.concatenate([vt_ref[hsl[h], :], ones_rows], axis=0)
        vw[h] = (vext[h].astype(F32) * jnp.exp(g_r - m_new[h])).astype(BF16)

    def phase_c(h):
        nd = w_inter[h] * qc[h] + _dot(vext[h], s_w[h])
        den = nd[M_DV:M_DV + 1]
        h_ref[hsl[h], :] = nd[:M_DV] / jnp.maximum(jnp.abs(den), jnp.exp(-m_t[h]))
        c_s[d, h] = decay[h] * cn[h] + _dot(vw[h], k_ref[:, hsl[h]])
        m_s[d, h] = jnp.broadcast_to(m_new[h], m_s.shape[2:])

    def all_heads(phase):
        def run():
            for h in range(M_HEADS):
                phase(h)
        return run

    return all_heads(phase_a), all_heads(phase_b), all_heads(phase_c)


def _mlstm_kernel(*refs, has_init, emit_state):
    fwd_in, bwd_in, gb_ref = refs[0:4], refs[4:8], refs[8]
    pos = 9
    if has_init:
        c0_ref, m0_ref = refs[pos:pos + 2]
        pos += 2
    h_refs = refs[pos:pos + 2]
    pos += 2
    if emit_state:
        cout_ref, mout_ref = refs[pos:pos + 2]
        pos += 2
    c_s, m_s = refs[pos:pos + 2]

    @pl.when(pl.program_id(1) == 0)
    def _():
        if has_init:
            c_s[...] = c0_ref[...]
            m_s[...] = m0_ref[...]
        else:
            c_s[...] = jnp.zeros_like(c_s)
            m_s[...] = jnp.zeros_like(m_s)

    a0, b0, c0 = _mlstm_direction(0, *fwd_in, gb_ref[0], h_refs[0], c_s, m_s)
    a1, b1, c1 = _mlstm_direction(1, *bwd_in, gb_ref[1], h_refs[1], c_s, m_s)
    for phase in (a0, a1, b0, c0, b1, c1):
        phase()

    if emit_state:
        cout_ref[...] = c_s[...]
        mout_ref[...] = m_s[...]


def _mlstm(p_main, p_gates, qvo_t, gate_b2, init, emit_state):
    b, t, _ = p_main.shape
    L = min(MLSTM_CHUNK, t)
    nc = t // L
    chunk = (lambda j: j, lambda j: nc - 1 - j)

    in_specs, args = [], []
    for d in range(2):
        in_specs += [
            pl.BlockSpec((None, M_WIDTH, L), lambda i, j, d=d: (i, T_Q // M_WIDTH, chunk[d](j))),
            pl.BlockSpec((None, L, M_WIDTH), lambda i, j, d=d: (i, chunk[d](j), P_K // M_WIDTH)),
            pl.BlockSpec((None, M_WIDTH, L), lambda i, j, d=d: (i, T_V // M_WIDTH, chunk[d](j))),
            pl.BlockSpec((None, L, LANES), lambda i, j, d=d: (i, chunk[d](j), d))]
        args += [qvo_t, p_main, qvo_t, p_gates]
    in_specs.append(pl.BlockSpec((2, 1, LANES), lambda i, j: (0, 0, 0)))
    args.append(gate_b2)
    st_c = pl.BlockSpec((None, 2, M_HEADS, NV, M_DK), lambda i, j: (i, 0, 0, 0, 0))
    st_m = pl.BlockSpec((None, 2, M_HEADS, 8, LANES), lambda i, j: (i, 0, 0, 0, 0))
    if init is not None:
        in_specs += [st_c, st_m]
        args += list(init)
    out_specs = [pl.BlockSpec((None, M_WIDTH, L), lambda i, j, d=d: (i, 0, chunk[d](j)))
                 for d in range(2)]
    out_shape = [jax.ShapeDtypeStruct((b, M_WIDTH, t), F32)] * 2
    if emit_state:
        out_specs += [st_c, st_m]
        out_shape += [jax.ShapeDtypeStruct((b, 2, M_HEADS, NV, M_DK), F32),
                      jax.ShapeDtypeStruct((b, 2, M_HEADS, 8, LANES), F32)]
    return pl.pallas_call(
        functools.partial(_mlstm_kernel, has_init=init is not None, emit_state=emit_state),
        grid=(b, nc),
        in_specs=in_specs,
        out_specs=out_specs,
        out_shape=out_shape,
        scratch_shapes=[pltpu.VMEM((2, M_HEADS, NV, M_DK), F32),
                        pltpu.VMEM((2, M_HEADS, 8, LANES), F32)],
        compiler_params=_cparams(("parallel", "arbitrary")),
        name="mlstm",
    )(*args)


def _mlaprep_kernel(cq_ref, ckv_ref, kr1_ref, kr2_ref, cos_ref, sin_ref, cost_ref, sint_ref,
                    qg_ref, kvg_ref, wq1_ref, wq2_ref, wk_ref, wv_ref, qt_ref, k_ref, vt_ref):
    cos = cos_ref[...]
    sin = sin_ref[...]
    cq = cq_ref[...].astype(F32)
    cqn = (_rms(cq) * qg_ref[...]).astype(BF16)
    ckv = ckv_ref[...].astype(F32)
    ckvn = (_rms(ckv) * kvg_ref[...]).astype(BF16)
    a_scale = (B_NOPE + B_ROPE) ** -0.5 * LOG2E
    kr = kr1_ref[...].astype(F32) * cos + kr2_ref[...].astype(F32) * sin
    cos_t = cost_ref[...] * a_scale
    sin_t = sint_ref[...] * a_scale
    ones = jnp.ones((V_EXT - B_V, vt_ref.shape[1]), vt_ref.dtype)
    v_all = _dot_nt(wv_ref[...], ckvn).astype(vt_ref.dtype)
    for h in range(B_HEADS):
        vt_ref[h * V_EXT:h * V_EXT + B_V, :] = v_all[h * B_V:(h + 1) * B_V]
        vt_ref[h * V_EXT + B_V:(h + 1) * V_EXT, :] = ones
    q1 = _dot_nt(wq1_ref[...], cqn)
    q2 = _dot_nt(wq2_ref[...], cqn)
    kn = _dot(ckvn, wk_ref[...])
    for h in range(B_HEADS):
        sl = slice(h * LANES, (h + 1) * LANES)
        qt_ref[sl, :] = (q1[sl] * cos_t + q2[sl] * sin_t).astype(qt_ref.dtype)
        k_ref[:, sl] = (kn[:, sl] + kr).astype(k_ref.dtype)


def _mlaprep(p_main, tables, qg, kvg, wq1, wq2, wk, wv, tm):
    cos, sin, cos_t, sin_t = tables
    b, t, _ = p_main.shape
    tm = min(tm, t)
    hw = B_HEADS * LANES

    def tok(width, cb):
        return pl.BlockSpec((None, tm, width), lambda i, j: (i, j, cb))

    def full(a):
        return pl.BlockSpec(a.shape, lambda i, j: (0,) * a.ndim)

    return pl.pallas_call(
        _mlaprep_kernel,
        grid=(b, t // tm),
        in_specs=[tok(B_Q_RANK, P_CQ // B_Q_RANK), tok(LANES, P_CKV // LANES),
                  tok(LANES, P_KR1 // LANES), tok(LANES, P_KR2 // LANES),
                  pl.BlockSpec((tm, LANES), lambda i, j: (j, 0)),
                  pl.BlockSpec((tm, LANES), lambda i, j: (j, 0)),
                  pl.BlockSpec((LANES, tm), lambda i, j: (0, j)),
                  pl.BlockSpec((LANES, tm), lambda i, j: (0, j)),
                  full(qg), full(kvg), full(wq1), full(wq2), full(wk), full(wv)],
        out_specs=[pl.BlockSpec((None, hw, tm), lambda i, j: (i, 0, j)),
                   pl.BlockSpec((None, tm, hw), lambda i, j: (i, j, 0)),
                   pl.BlockSpec((None, B_HEADS * V_EXT, tm), lambda i, j: (i, 0, j))],
        out_shape=[jax.ShapeDtypeStruct((b, hw, t), BF16),
                   jax.ShapeDtypeStruct((b, t, hw), BF16),
                   jax.ShapeDtypeStruct((b, B_HEADS * V_EXT, t), BF16)],
        compiler_params=_cparams(("parallel", "parallel")),
        name="mlaprep",
    )(p_main, p_main, p_main, p_main, cos, sin, cos_t, sin_t, qg, kvg, wq1, wq2, wk, wv)


def _flash_kernel(*refs, tk, nk, has_ctx):
    if has_ctx:
        q_ref, k_ref, v_ref, kc_ref, vc_ref, o_ref, s_s, p_s = refs
    else:
        q_ref, k_ref, v_ref, o_ref, s_s, p_s = refs
    tq = q_ref.shape[1]
    sls = [slice(h * LANES, (h + 1) * LANES) for h in range(2)]
    vsl = [slice(h * V_EXT, (h + 1) * V_EXT) for h in range(2)]

    def stage_a(h, slot, kb):
        s = _dot(kb, q_ref[sls[h], :])
        s_s[h, slot, 0:kb.shape[0]] = s
        return jnp.max(s, axis=0, keepdims=True)

    def stage_b(h, slot, rows, mx, m):
        m_new = jnp.maximum(m, mx)
        p_s[h, slot, 0:rows] = jnp.exp2(s_s[h, slot, 0:rows] - m_new).astype(BF16)
        return m_new, jnp.exp2(m - m_new)

    def stage_c(h, slot, vb, alpha, acc):
        return alpha * acc + _dot(vb, p_s[h, slot, 0:vb.shape[1]])

    def kblock(j, h):
        return k_ref[pl.ds(pl.multiple_of(j * tk, tk), tk), sls[h]]

    def vblock(j, h):
        return v_ref[vsl[h], pl.ds(pl.multiple_of(j * tk, LANES), tk)]

    def trip(slot, state, k_new, v_old):
        new = []
        for h in range(2):
            mx, m, alpha, acc = state[h]
            mx_new = stage_a(h, slot, k_new(h))
            m, alpha_new = stage_b(h, 1 - slot, tk, mx, m)
            acc = stage_c(h, slot, v_old(h), alpha, acc)
            new.append((mx_new, m, alpha_new, acc))
        return tuple(new)

    zrow = jnp.zeros((1, tq), F32)
    neg = jnp.full((1, tq), NEG, F32)
    zacc = jnp.zeros((V_EXT, tq), F32)
    state = []
    for h in range(2):
        mx = stage_a(h, 0, k_ref[0:tk, sls[h]])
        if nk > 1:
            mx1 = stage_a(h, 1, k_ref[tk:2 * tk, sls[h]])
            m, alpha = stage_b(h, 0, tk, mx, neg)
            state.append((mx1, m, alpha, zacc))
        else:
            state.append((mx, neg, zrow, zacc))
    state = tuple(state)
    if nk > 2:
        assert nk % 2 == 0

        def body(i, state):
            for par in range(2):
                j = 2 * i + par
                state = trip(par, state, functools.partial(kblock, j),
                             functools.partial(vblock, j - 2))
            return state

        state = lax.fori_loop(1, nk // 2, body, state)
    last = (nk - 1) % 2
    tail_rows = tk
    if has_ctx:
        assert nk > 1 and kc_ref.shape[0] <= tk
        state = trip(1 - last, state, lambda h: kc_ref[:, sls[h]],
                     lambda h: v_ref[vsl[h], (nk - 2) * tk:(nk - 1) * tk])
        tail_rows = kc_ref.shape[0]
    outs = []
    for h in range(2):
        mx, m, alpha, acc = state[h]
        if has_ctx:
            acc = stage_c(h, last, v_ref[vsl[h], (nk - 1) * tk:nk * tk], alpha, acc)
            m, alpha = stage_b(h, 1 - last, tail_rows, mx, m)
            acc = stage_c(h, 1 - last, vc_ref[vsl[h], :], alpha, acc)
        else:
            if nk > 1:
                acc = stage_c(h, 1 - last, v_ref[vsl[h], (nk - 2) * tk:(nk - 1) * tk], alpha, acc)
            m, alpha = stage_b(h, last, tail_rows, mx, m)
            acc = stage_c(h, last, v_ref[vsl[h], (nk - 1) * tk:nk * tk], alpha, acc)
        outs.append(acc[0:B_V] / acc[B_V:B_V + 1])
    o_ref[...] = jnp.concatenate(outs, axis=0).T.astype(o_ref.dtype)


def _flash(q_t, k, v_t, kc, vc_t, tq, tk):
    b, _, t = q_t.shape
    tkeys = k.shape[1]
    tq = min(tq, t)
    tk = min(tk, tkeys)
    assert tkeys % tk == 0
    has_ctx = kc is not None
    in_specs = [pl.BlockSpec((None, 2 * LANES, tq), lambda i, hp, j: (i, hp, j)),
                pl.BlockSpec((None, tkeys, 2 * LANES), lambda i, hp, j: (i, 0, hp)),
                pl.BlockSpec((None, 2 * V_EXT, tkeys), lambda i, hp, j: (i, hp, 0))]
    args = [q_t, k, v_t]
    if has_ctx:
        tc = kc.shape[1]
        in_specs += [pl.BlockSpec((None, tc, 2 * LANES), lambda i, hp, j: (i, 0, hp)),
                     pl.BlockSpec((None, 2 * V_EXT, tc), lambda i, hp, j: (i, hp, 0))]
        args += [kc, vc_t]
    return pl.pallas_call(
        functools.partial(_flash_kernel, tk=tk, nk=tkeys // tk, has_ctx=has_ctx),
        grid=(b, B_HEADS // 2, t // tq),
        in_specs=in_specs,
        out_specs=pl.BlockSpec((None, tq, LANES), lambda i, hp, j: (i, j, hp)),
        out_shape=jax.ShapeDtypeStruct((b, t, B_WIDTH), BF16),
        scratch_shapes=[pltpu.VMEM((2, 2, tk, tq), F32), pltpu.VMEM((2, 2, tk, tq), BF16)],
        compiler_params=_cparams(("parallel", "parallel", "arbitrary")),
        name="flash",
    )(*args)


NA_PAIR_ROWS = NA_KH + 2
NA_PAIR_KEYS = NA_PAIR_ROWS * GRID_W
NA_VARIANTS = 5


def _natten_pair_window(r, rows):
    return jnp.clip(r - NA_KH // 2, 0, rows - NA_PAIR_ROWS)


def _natten_kernel(qt_ref, k_ref, vt_ref, kc_ref, vct_ref, bias_ref, o_ref, *, rb, rows):
    blk = pl.program_id(2)
    tq = rb * GRID_W
    pq = 2 * GRID_W
    npair = rb // 2
    q_t = qt_ref[...]
    feat = lax.broadcasted_iota(jnp.int32, q_t.shape, 0)
    zero = jnp.zeros_like(q_t)
    qh = [jnp.where(feat < C_HEAD_DIM, q_t, zero), jnp.where(feat >= C_HEAD_DIM, q_t, zero)]
    q2 = [jnp.concatenate([qh[0][:, i * pq:(i + 1) * pq], qh[1][:, i * pq:(i + 1) * pq]], axis=1)
          for i in range(npair)]
    s_c = _dot(kc_ref[...], jnp.concatenate(q2, axis=1))
    m_c = jnp.max(s_c, axis=0, keepdims=True)

    def scores(i):
        r = blk * rb + 2 * i
        ws = _natten_pair_window(r, rows)
        koff = pl.multiple_of(ws * GRID_W, LANES)
        s_w = _dot(k_ref[pl.ds(koff, NA_PAIR_KEYS), :], q2[i])
        var = (r - ws) // 2
        return koff, s_w + jnp.concatenate([bias_ref[0, var], bias_ref[1, var]], axis=1)

    def softmax(i, s_w):
        csl = slice(i * 2 * pq, (i + 1) * 2 * pq)
        m = jnp.maximum(jnp.max(s_w, axis=0, keepdims=True), m_c[:, csl])
        p_w = jnp.exp2(s_w - m)
        p_ci = jnp.exp2(s_c[:, csl] - m)
        l = jnp.sum(p_w, axis=0, keepdims=True) + jnp.sum(p_ci, axis=0, keepdims=True)
        return p_w.astype(BF16), p_ci.astype(BF16), l

    o_w, p_c, l_all = [], [], []
    sc = {0: scores(0)}
    if npair > 1:
        sc[1] = scores(1)
    sm = {0: softmax(0, sc[0][1])}
    for i in range(npair):
        if i + 2 < npair:
            sc[i + 2] = scores(i + 2)
        if i + 1 < npair:
            sm[i + 1] = softmax(i + 1, sc[i + 1][1])
        p_w, p_ci, l = sm.pop(i)
        koff = sc.pop(i)[0]
        p_c.append(p_ci)
        l_all.append(l)
        o_w.append(_dot(vt_ref[:, pl.ds(koff, NA_PAIR_KEYS)], p_w))
    o2 = ((jnp.concatenate(o_w, axis=1) + _dot(vct_ref[...], jnp.concatenate(p_c, axis=1)))
          / jnp.concatenate(l_all, axis=1))
    top = lax.broadcasted_iota(jnp.int32, (LANES, pq), 0) < C_HEAD_DIM
    out_t = jnp.concatenate(
        [jnp.where(top, o2[:, i * 2 * pq:i * 2 * pq + pq], o2[:, i * 2 * pq + pq:(i + 1) * 2 * pq])
         for i in range(npair)], axis=1)
    o_ref[...] = out_t.T.astype(o_ref.dtype)


def _natten(qv_t, k, kc, vc_t, bias_t, rb):
    b, t, _ = k.shape
    tc = kc.shape[1]
    rows = t // GRID_W
    rb = min(rb, rows)
    assert rows % 2 == 0 and rows >= NA_PAIR_ROWS + 4 and rb % 2 == 0
    nhp = C_HEADS // 2
    return pl.pallas_call(
        functools.partial(_natten_kernel, rb=rb, rows=rows),
        grid=(b, nhp, rows // rb),
        in_specs=[pl.BlockSpec((None, LANES, rb * GRID_W), lambda i, hp, j: (i, hp, j)),
                  pl.BlockSpec((None, t, LANES), lambda i, hp, j: (i, 0, hp)),
                  pl.BlockSpec((None, LANES, t), lambda i, hp, j: (i, nhp + hp, 0)),
                  pl.BlockSpec((None, tc, LANES), lambda i, hp, j: (i, 0, hp)),
                  pl.BlockSpec((None, LANES, tc), lambda i, hp, j: (i, hp, 0)),
                  pl.BlockSpec((None,) + bias_t.shape[1:], lambda i, hp, j: (hp, 0, 0, 0, 0))],
        out_specs=pl.BlockSpec((None, rb * GRID_W, LANES), lambda i, hp, j: (i, j, hp)),
        out_shape=jax.ShapeDtypeStruct((b, t, C_HEADS * C_HEAD_DIM), BF16),
        compiler_params=_cparams(("parallel", "parallel", "arbitrary")),
        name="natten",
    )(qv_t, k, qv_t, kc, vc_t, bias_t)


def _natten_bias_table(rel_bias):
    kcol = np.arange(GRID_W)[:, None]
    qc = np.arange(GRID_W)[None, :]
    cs = np.clip(qc - NA_KW // 2, 0, GRID_W - NA_KW)
    valid = (kcol >= cs) & (kcol < cs + NA_KW)
    pad = GRID_W - NA_KW
    rev = jnp.pad(rel_bias.astype(F32) * LOG2E, ((0, 0), (0, 0), (pad, pad)))[:, :, ::-1]
    toe = jnp.stack([rev[:, :, GRID_W - 1 - k:2 * GRID_W - 1 - k] for k in range(GRID_W)], axis=2)
    toe = jnp.where(valid[None, None], toe, NEG)
    nri = 2 * NA_KH - 1
    neg = jnp.full_like(toe[:, :1], NEG)
    up = jnp.concatenate([toe, neg], axis=1)
    down = jnp.concatenate([neg, toe], axis=1)
    negs = jnp.full_like(up, NEG)
    both = jnp.concatenate([up, down], axis=-1)
    left = jnp.concatenate([up, negs], axis=-1)
    right = jnp.concatenate([negs, down], axis=-1)
    none = jnp.full_like(both[:, 0], NEG)
    own = ((0, 0), (0, 0), (0, 1), (2, 2), (2, 2))
    variants = []
    for v in range(NA_VARIANTS):
        key_rows = []
        for x in range(NA_PAIR_ROWS):
            ri0 = x - 2 * v + NA_KH - 1
            has = [own[v][j] <= x < own[v][j] + NA_KH for j in range(2)]
            assert not any(has) or 0 <= ri0 <= nri
            src = both if all(has) else left if has[0] else right if has[1] else None
            key_rows.append(none if src is None else src[:, ri0])
        variants.append(jnp.concatenate(key_rows, axis=1))
    tab = jnp.stack(variants, axis=1)
    return tab.reshape(C_HEADS // 2, 2, NA_VARIANTS, NA_PAIR_KEYS, 2 * GRID_W)


def _mlp_tail(x1, g2, sh2, sc2, n2g, w1_ref, w2_ref):
    h = (_rms(x1) * n2g * (1.0 + sc2) + sh2).astype(BF16)
    acc = jnp.zeros_like(x1)
    chunk = 1024
    for c0 in range(0, D_FF, chunk):
        a = jnp.maximum(_dot(h, w1_ref[:, c0:c0 + chunk]), 0.0)
        acc = acc + _dot((a * a).astype(BF16), w2_ref[c0:c0 + chunk, :])
    return x1 + g2 * acc


def _post_ab_kernel(x_ref, hf_ref, hb_ref, og_ref, bl_ref, mg_ref, g1_ref, sh2_ref, sc2_ref,
                    g2_ref, n2g_ref, wo_ref, w1_ref, w2_ref, o_ref):
    hm = hf_ref[...] + hb_ref[...]
    og = jax.nn.sigmoid(og_ref[...].astype(F32))
    mg = mg_ref[...]
    y = _dot(bl_ref[...], wo_ref[M_WIDTH:, :])
    for h in range(M_HEADS):
        sl = slice(h * M_DV, (h + 1) * M_DV)
        xh = hm[sl]
        inv = lax.rsqrt(jnp.mean(xh * xh, axis=0, keepdims=True) + EPS)
        hn = (xh * inv * mg[sl] * og[sl]).astype(BF16)
        y = y + lax.dot_general(hn, wo_ref[sl, :], (((0,), (0,)), ((), ())),
                                preferred_element_type=F32)
    x1 = x_ref[...] + g1_ref[...] * y
    o_ref[...] = _mlp_tail(x1, g2_ref[...], sh2_ref[...], sc2_ref[...], n2g_ref[...],
                           w1_ref, w2_ref)


def _post_c_kernel(x_ref, y_ref, g1_ref, sh2_ref, sc2_ref, g2_ref, n2g_ref, fg_ref,
                   wo_ref, w1_ref, w2_ref, o_ref):
    x1 = x_ref[...] + g1_ref[...] * _dot(y_ref[...], wo_ref[...])
    x2 = _mlp_tail(x1, g2_ref[...], sh2_ref[...], sc2_ref[...], n2g_ref[...], w1_ref, w2_ref)
    o_ref[...] = _rms(x2) * fg_ref[...]


def _resident(a):
    return pl.BlockSpec(a.shape, lambda i, j: (0,) * a.ndim, pipeline_mode=pl.Buffered(1))


def _post_ab(x, h_dirs, qvo_t, b_lat, mg, mods, n2g, wo, w1, w2, tm):
    b, t, d = x.shape
    tm = min(tm, t)
    g1, sh2, sc2, g2 = mods
    bm = g1.shape[0]
    mod_map = (lambda i, j: (i, 0, 0)) if bm == b else (lambda i, j: (0, 0, 0))
    mod_spec = pl.BlockSpec((None, 1, d), mod_map)
    tok = lambda w, cb: pl.BlockSpec((None, tm, w), lambda i, j: (i, j, cb))
    return pl.pallas_call(
        _post_ab_kernel,
        grid=(b, t // tm),
        in_specs=[tok(d, 0),
                  pl.BlockSpec((None, M_WIDTH, tm), lambda i, j: (i, 0, j)),
                  pl.BlockSpec((None, M_WIDTH, tm), lambda i, j: (i, 0, j)),
                  pl.BlockSpec((None, M_WIDTH, tm), lambda i, j: (i, T_O // M_WIDTH, j)),
                  tok(B_WIDTH, 0),
                  pl.BlockSpec((M_WIDTH, 1), lambda i, j: (0, 0)),
                  mod_spec, mod_spec, mod_spec, mod_spec,
                  pl.BlockSpec((1, d), lambda i, j: (0, 0)),
                  _resident(wo), _resident(w1), _resident(w2)],
        out_specs=tok(d, 0),
        out_shape=jax.ShapeDtypeStruct((b, t, d), F32),
        compiler_params=_cparams(("parallel", "parallel")),
        name="post_ab",
    )(x, h_dirs[0], h_dirs[1], qvo_t, b_lat, mg, g1, sh2, sc2, g2, n2g.reshape(1, d), wo, w1, w2)


def _post_c(x, y, mods, n2g, fg, wo, w1, w2, tm):
    b, t, d = x.shape
    tm = min(tm, t)
    g1, sh2, sc2, g2 = mods
    mod_spec = pl.BlockSpec((None, 1, d), lambda i, j: (i, 0, 0))
    tok = pl.BlockSpec((None, tm, d), lambda i, j: (i, j, 0))
    vec = pl.BlockSpec((1, d), lambda i, j: (0, 0))
    return pl.pallas_call(
        _post_c_kernel,
        grid=(b, t // tm),
        in_specs=[tok, tok, mod_spec, mod_spec, mod_spec, mod_spec, vec, vec,
                  _resident(wo), _resident(w1), _resident(w2)],
        out_specs=tok,
        out_shape=jax.ShapeDtypeStruct((b, t, d), F32),
        compiler_params=_cparams(("parallel", "parallel")),
        name="post_c",
    )(x, y, g1, sh2, sc2, g2, n2g.reshape(1, d), fg.reshape(1, d), wo, w1, w2)


def _rope_tables(t):
    pos = np.arange(t)
    row = (pos // GRID_W).astype(np.float32)
    colp = (pos % GRID_W).astype(np.float32)
    n_f = B_ROPE // 4
    freqs = (ROPE_BASE ** (-np.arange(n_f, dtype=np.float32) / n_f)).astype(np.float32)
    ang = np.concatenate([row[:, None] * freqs, colp[:, None] * freqs], axis=-1)
    cos = np.zeros((t, LANES), np.float32)
    sin = np.zeros((t, LANES), np.float32)
    cos[:, :B_NOPE] = 1.0
    cos[:, B_NOPE:B_NOPE + 16] = np.cos(ang)
    cos[:, B_NOPE + 16:B_NOPE + 32] = np.cos(ang)
    sin[:, B_NOPE:B_NOPE + 16] = np.sin(ang)
    sin[:, B_NOPE + 16:B_NOPE + 32] = np.sin(ang)
    return tuple(jnp.asarray(a) for a in (cos, sin, cos.T.copy(), sin.T.copy()))


def _identity_tables(t):
    cos = np.zeros((t, LANES), np.float32)
    cos[:, :B_NOPE + B_ROPE] = 1.0
    sin = np.zeros((t, LANES), np.float32)
    return tuple(jnp.asarray(a) for a in (cos, sin, cos.T.copy(), sin.T.copy()))


def _layer0_weights(w_in, gate_b, w_uq, w_ukv):
    d = w_in.shape[0]
    half = B_ROPE // 2
    o_g = 4 * M_WIDTH
    o_cq = o_g + 4 * M_HEADS
    o_ckv = o_cq + B_Q_RANK
    o_kr = o_ckv + B_KV_RANK
    kr = w_in[:, o_kr:o_kr + B_ROPE]
    z = lambda n: jnp.zeros((d, n), w_in.dtype)
    kr1 = jnp.concatenate([z(B_NOPE), kr, z(LANES - B_NOPE - B_ROPE)], axis=1)
    kr2 = jnp.concatenate([z(B_NOPE), -kr[:, half:], kr[:, :half], z(LANES - B_NOPE - B_ROPE)], axis=1)
    wg = w_in[:, o_g:o_g + 4 * M_HEADS]
    gpad = z(LANES - 2 * M_HEADS)
    mq, mk, mv, mo = (w_in[:, i * M_WIDTH:(i + 1) * M_WIDTH] for i in range(4))
    w0 = jnp.concatenate([mk, w_in[:, o_cq:o_kr], kr1, kr2,
                          wg[:, :2 * M_HEADS], gpad, wg[:, 2 * M_HEADS:], gpad], axis=1).astype(BF16)
    w0_t = jnp.concatenate([mq, mv, mo], axis=1).T.astype(BF16)
    gb = jnp.zeros((2, 1, LANES), F32)
    gb = gb.at[0, 0, :2 * M_HEADS].set(gate_b[:2 * M_HEADS]).at[1, 0, :2 * M_HEADS].set(gate_b[2 * M_HEADS:])

    qr = B_Q_RANK
    wq = w_uq.reshape(qr, B_HEADS, B_NOPE + B_ROPE)
    nope, ra, rb = wq[..., :B_NOPE], wq[..., B_NOPE:B_NOPE + half], wq[..., B_NOPE + half:]
    zq = jnp.zeros((qr, B_HEADS, LANES - B_NOPE - B_ROPE), w_uq.dtype)
    wq1 = jnp.concatenate([nope, ra, rb, zq], axis=-1).reshape(qr, B_HEADS * LANES).astype(BF16)
    wq2 = jnp.concatenate([jnp.zeros_like(nope), -rb, ra, zq], axis=-1).reshape(qr, B_HEADS * LANES).astype(BF16)
    wkv = w_ukv.reshape(B_KV_RANK, B_HEADS, B_NOPE + B_V)
    wk = jnp.concatenate([wkv[..., :B_NOPE], jnp.zeros((B_KV_RANK, B_HEADS, LANES - B_NOPE), w_ukv.dtype)],
                         axis=-1).reshape(B_KV_RANK, B_HEADS * LANES).astype(BF16)
    wv = wkv[..., B_NOPE:].reshape(B_KV_RANK, B_WIDTH).astype(BF16)
    return w0, w0_t, gb, wq1.T, wq2.T, wk, wv.T


def _split_mod(mod_rows):
    return [m[:, None, :] for m in jnp.split(mod_rows, 6, axis=-1)]


def kernel(x, c, ctx, c_ctx, ada_w, ada_b, norm1_g, norm2_g, mlp_w1, mlp_w2, ab_w_in, ab_gate_b,
           ab_m_norm_g, ab_q_norm_g, ab_kv_norm_g, ab_w_uq, ab_w_ukv, ab_w_out, na_w_in,
           na_rel_bias, na_w_out, final_norm_g):
    b, t, d = x.shape
    tc = ctx.shape[1]
    c_rows = jnp.concatenate([c, c_ctx[None, :], jnp.zeros((8 - b - 1, d), F32)], axis=0)
    mod = _ada(c_rows, ada_w, ada_b)

    sh1, sc1, g1, sh2, sc2, g2 = _split_mod(mod[0, :b])
    csh1, csc1, cg1, csh2, csc2, cg2 = _split_mod(mod[0, b:b + 1])
    w0, w0_t, gb, wq1, wq2, wk, wv = _layer0_weights(ab_w_in[0], ab_gate_b[0], ab_w_uq[0],
                                                      ab_w_ukv[0])
    widths, dtypes = (P_MAIN, P_GATES), (BF16, F32)
    p_main, p_gates, qvo_t = _modproj(x, sh1, sc1, norm1_g[0], w0, widths, dtypes, tm=512, w_t=w0_t)
    pc_main, pc_gates, qvo_ct = _modproj(ctx, csh1, csc1, norm1_g[0], w0, widths, dtypes, tm=256,
                                         w_t=w0_t)

    hc_f, hc_b, st_c, st_m = _mlstm(pc_main, pc_gates, qvo_ct, gb, None, True)
    h_dirs = _mlstm(p_main, p_gates, qvo_t, gb, (st_c, st_m), False)
    hc_dirs = (hc_f, hc_b)

    qg = ab_q_norm_g[0].reshape(1, -1)
    kvg = ab_kv_norm_g[0].reshape(1, -1)
    q_l, k_l, v_l = _mlaprep(p_main, _rope_tables(t), qg, kvg, wq1, wq2, wk, wv, tm=512)
    q_c, k_c, v_c = _mlaprep(pc_main, _identity_tables(tc), qg, kvg, wq1, wq2, wk, wv, tm=256)
    b_lat = _flash(q_l, k_l, v_l, k_c, v_c, tq=2048, tk=512)
    b_ctx = _flash(q_c, k_c, v_c, None, None, tq=256, tk=tc)

    wo = ab_w_out[0].astype(BF16)
    w1 = mlp_w1[0].astype(BF16)
    w2 = mlp_w2[0].astype(BF16)
    mg = ab_m_norm_g[0].reshape(M_WIDTH, 1)
    x = _post_ab(x, h_dirs, qvo_t, b_lat, mg, (g1, sh2, sc2, g2), norm2_g[0], wo, w1, w2, tm=512)
    ctx = _post_ab(ctx, hc_dirs, qvo_ct, b_ctx, mg, (cg1, csh2, csc2, cg2), norm2_g[0],
                   wo, w1, w2, tm=256)

    sh1, sc1, g1, sh2, sc2, g2 = _split_mod(mod[1, :b])
    csh1, csc1 = _split_mod(mod[1, b:b + 1])[:2]
    cw = C_HEADS * C_HEAD_DIM
    w_q = na_w_in[0][:, :cw] * (C_HEAD_DIM ** -0.5 * LOG2E)
    w_k = na_w_in[0][:, cw:2 * cw].astype(BF16)
    w_qv_t = jnp.concatenate([w_q, na_w_in[0][:, 2 * cw:]], axis=1).T.astype(BF16)
    k_n, qv_t = _modproj(x, sh1, sc1, norm1_g[1], w_k, (cw,), (BF16,), tm=512, w_t=w_qv_t)
    kc_n, vc_t = _modproj(ctx, csh1, csc1, norm1_g[1], w_k, (cw,), (BF16,), tm=256,
                          w_t=w_qv_t[cw:])
    bias_t = _natten_bias_table(na_rel_bias[0])
    y = _natten(qv_t, k_n, kc_n, vc_t, bias_t, rb=64)
    return _post_c(x, y, (g1, sh2, sc2, g2), norm2_g[1], final_norm_g,
                   na_w_out[0].astype(BF16), mlp_w1[1].astype(BF16), mlp_w2[1].astype(BF16), tm=512)
```

```python
import functools

import numpy as np
import jax
import jax.numpy as jnp
from jax import lax
from jax.experimental import pallas as pl
from jax.experimental.pallas import tpu as pltpu

F32 = jnp.float32
BF16 = jnp.bfloat16

D_MODEL = 1024
DEPTH = 2
GRID_W = 64
EPS = 1e-6
M_HEADS = 4
M_DK = 128
M_DV = 128
B_HEADS = 8
B_Q_RANK = 256
B_KV_RANK = 128
B_NOPE = 64
B_ROPE = 32
B_V = 64
ROPE_BASE = 10000.0
C_HEADS = 16
C_HEAD_DIM = 64
NA_KH = 8
NA_KW = 16
D_FF = 4 * D_MODEL
M_WIDTH = M_HEADS * M_DV
B_WIDTH = B_HEADS * B_V

LANES = 128
NEG = -1e30
LOG2E = float(np.log2(np.e))
VMEM_LIMIT = 56 * 1024 * 1024

P_K, P_CQ, P_CKV, P_KR1, P_KR2 = 0, 512, 768, 896, 1024
P_MAIN = 1152
T_Q, T_V, T_O = 0, 512, 1024
P_GATES = 256
NV = 2 * M_DV

TOKEN_TILE = 512
MLSTM_CHUNK = 256
FLASH_TQ = 2048
FLASH_TK = 512
NATTEN_ROWS = 64
BF16_ROWS = 16
V_EXT = B_V + BF16_ROWS


def _cparams(sem):
    return pltpu.CompilerParams(dimension_semantics=sem, vmem_limit_bytes=VMEM_LIMIT)


def _rms(x):
    return x * lax.rsqrt(jnp.mean(x * x, axis=-1, keepdims=True) + EPS)


def _dot(a, b):
    return jnp.dot(a, b, preferred_element_type=F32)


def _dot_nt(a, b):
    return lax.dot_general(a, b, (((1,), (1,)), ((), ())), preferred_element_type=F32)


def _ada_kernel(c_ref, w_ref, b_ref, o_ref):
    c = c_ref[...]
    s = c * jax.nn.sigmoid(c)
    o_ref[...] = jnp.dot(s, w_ref[...], precision=lax.Precision.HIGHEST,
                         preferred_element_type=F32) + b_ref[...]


def _ada(c_rows, ada_w, ada_b):
    depth, d, n = ada_w.shape
    tn = 1536
    return pl.pallas_call(
        _ada_kernel,
        grid=(depth, n // tn),
        in_specs=[pl.BlockSpec((8, d), lambda l, j: (0, 0)),
                  pl.BlockSpec((None, d, tn), lambda l, j: (l, 0, j)),
                  pl.BlockSpec((None, 1, tn), lambda l, j: (l, 0, j))],
        out_specs=pl.BlockSpec((None, 8, tn), lambda l, j: (l, 0, j)),
        out_shape=jax.ShapeDtypeStruct((depth, 8, n), F32),
        compiler_params=_cparams(("arbitrary", "arbitrary")),
        name="ada",
    )(c_rows, ada_w, ada_b.reshape(depth, 1, n))


def _modproj_kernel(x_ref, sh_ref, sc_ref, g_ref, w_ref, *refs, widths, has_t, chunk):
    wt_ref = refs[0] if has_t else None
    o_refs = refs[1:] if has_t else refs
    x = x_ref[...]
    h = _rms(x) * g_ref[...] * (1.0 + sc_ref[...]) + sh_ref[...]
    hb = h.astype(BF16)
    off = 0
    for o_ref, n in zip(o_refs, widths):
        for c0 in range(0, n, chunk):
            c1 = min(n, c0 + chunk)
            o_ref[:, c0:c1] = _dot(hb, w_ref[:, off + c0:off + c1]).astype(o_ref.dtype)
        off += n
    if has_t:
        ot_ref = o_refs[-1]
        for c0 in range(0, wt_ref.shape[0], chunk):
            ot_ref[c0:c0 + chunk, :] = _dot_nt(wt_ref[c0:c0 + chunk, :], hb).astype(ot_ref.dtype)


def _modproj(x, shift, scale, g, w, widths, dtypes, tm, w_t=None):
    b, t, d = x.shape
    bm = shift.shape[0]
    mod_map = (lambda i, j: (i, 0, 0)) if bm == b else (lambda i, j: (0, 0, 0))
    tm = min(tm, t)
    in_specs = [pl.BlockSpec((None, tm, d), lambda i, j: (i, j, 0)),
                pl.BlockSpec((None, 1, d), mod_map),
                pl.BlockSpec((None, 1, d), mod_map),
                pl.BlockSpec((1, d), lambda i, j: (0, 0)),
                pl.BlockSpec(w.shape, lambda i, j: (0, 0))]
    args = [x, shift, scale, g.reshape(1, d), w]
    out_specs = [pl.BlockSpec((None, tm, n), lambda i, j: (i, j, 0)) for n in widths]
    out_shape = [jax.ShapeDtypeStruct((b, t, n), dt) for n, dt in zip(widths, dtypes)]
    if w_t is not None:
        in_specs.append(pl.BlockSpec(w_t.shape, lambda i, j: (0, 0)))
        args.append(w_t)
        out_specs.append(pl.BlockSpec((None, w_t.shape[0], tm), lambda i, j: (i, 0, j)))
        out_shape.append(jax.ShapeDtypeStruct((b, w_t.shape[0], t), BF16))
    return pl.pallas_call(
        functools.partial(_modproj_kernel, widths=widths, has_t=w_t is not None, chunk=512),
        grid=(b, t // tm),
        in_specs=in_specs,
        out_specs=out_specs,
        out_shape=out_shape,
        compiler_params=_cparams(("parallel", "parallel")),
        name="modproj",
    )(*args)


def _mlstm_direction(d, qt_ref, k_ref, vt_ref, g_ref, gb, h_ref, c_s, m_s):
    L = k_ref.shape[0]
    r = lax.broadcasted_iota(jnp.int32, (L, L), 0)
    c = lax.broadcasted_iota(jnp.int32, (L, L), 1)
    keep = (r >= c) if d == 0 else (r <= c)
    keep_t = (c >= r) if d == 0 else (c <= r)

    gates = g_ref[...] + gb
    ls = jax.nn.log_sigmoid(gates)
    gates_t = gates.T[0:8]
    ls_t = jax.nn.log_sigmoid(gates_t)

    def split3(a):
        hi = a.astype(BF16)
        mid = (a - hi.astype(F32)).astype(BF16)
        lo = (a - hi.astype(F32) - mid.astype(F32)).astype(BF16)
        return hi, mid, lo

    trib = keep.astype(BF16)
    bc3 = _dot(trib, jnp.concatenate(split3(ls), axis=1))
    b_cols = bc3[:, 0:LANES] + bc3[:, LANES:2 * LANES] + bc3[:, 2 * LANES:]
    br3 = _dot_nt(jnp.concatenate(split3(ls_t), axis=0), trib)
    b_rows = br3[0:8] + br3[8:16] + br3[16:24]
    b_tot = jnp.sum(ls, axis=0, keepdims=True)

    ones_rows = jnp.ones((M_DV, L), BF16)
    scale = M_DK ** -0.5
    hsl = [slice(h * M_DK, (h + 1) * M_DK) for h in range(M_HEADS)]

    cn, m, s_raw, qc = {}, {}, {}, {}
    s_w, w_inter, m_t, vext, vw, decay, m_new = {}, {}, {}, {}, {}, {}, {}

    def phase_a(h):
        cn[h] = c_s[d, h]
        m[h] = m_s[d, h][0:1, 0:1]
        lhs = jnp.concatenate([k_ref[:, hsl[h]], cn[h].astype(BF16)], axis=0)
        sq = _dot(lhs, qt_ref[hsl[h], :])
        s_raw[h] = sq[0:L]
        qc[h] = sq[L:]

    def phase_b(h):
        col = gates[:, h:h + 1] - b_cols[:, 4 + h:5 + h]
        li_r = gates_t[h:h + 1, :]
        b_r = b_rows[4 + h:5 + h, :]
        b_last = b_tot[:, 4 + h:5 + h]
        dm = jnp.where(keep_t, b_r + col, NEG)
        inter = b_r + m[h]
        m_t[h] = jnp.maximum(jnp.max(dm, axis=0, keepdims=True), inter)
        s_w[h] = (s_raw[h] * (scale * jnp.exp(dm - m_t[h]))).astype(BF16)
        w_inter[h] = jnp.exp(inter - m_t[h]) * scale
        g_r = b_last - b_r + li_r
        m_new[h] = jnp.maximum(b_last + m[h], jnp.max(g_r, axis=1, keepdims=True))
        decay[h] = jnp.exp(b_last + m[h] - m_new[h])
        vext[h] = jnp.concatenate([vt_ref[hsl[h], :], ones_rows], axis=0)
        vw[h] = (vext[h].astype(F32) * jnp.exp(g_r - m_new[h])).astype(BF16)

    def phase_c(h):
        nd = w_inter[h] * qc[h] + _dot(vext[h], s_w[h])
        den = nd[M_DV:M_DV + 1]
        h_ref[hsl[h], :] = nd[:M_DV] / jnp.maximum(jnp.abs(den), jnp.exp(-m_t[h]))
        c_s[d, h] = decay[h] * cn[h] + _dot(vw[h], k_ref[:, hsl[h]])
        m_s[d, h] = jnp.broadcast_to(m_new[h], m_s.shape[2:])

    def all_heads(phase):
        def run():
            for h in range(M_HEADS):
                phase(h)
        return run

    return all_heads(phase_a), all_heads(phase_b), all_heads(phase_c)


def _mlstm_kernel(*refs, has_init, emit_state):
    fwd_in, bwd_in, gb_ref = refs[0:4], refs[4:8], refs[8]
    pos = 9
    if has_init:
        c0_ref, m0_ref = refs[pos:pos + 2]
        pos += 2
    h_refs = refs[pos:pos + 2]
    pos += 2
    if emit_state:
        cout_ref, mout_ref = refs[pos:pos + 2]
        pos += 2
    c_s, m_s = refs[pos:pos + 2]

    @pl.when(pl.program_id(1) == 0)
    def _():
        if has_init:
            c_s[...] = c0_ref[...]
            m_s[...] = m0_ref[...]
        else:
            c_s[...] = jnp.zeros_like(c_s)
            m_s[...] = jnp.zeros_like(m_s)

    a0, b0, c0 = _mlstm_direction(0, *fwd_in, gb_ref[0], h_refs[0], c_s, m_s)
    a1, b1, c1 = _mlstm_direction(1, *bwd_in, gb_ref[1], h_refs[1], c_s, m_s)
    for phase in (a0, a1, b0, c0, b1, c1):
        phase()

    if emit_state:
        cout_ref[...] = c_s[...]
        mout_ref[...] = m_s[...]


def _mlstm(p_main, p_gates, qvo_t, gate_b2, init, emit_state):
    b, t, _ = p_main.shape
    L = min(MLSTM_CHUNK, t)
    nc = t // L
    chunk = (lambda j: j, lambda j: nc - 1 - j)

    in_specs, args = [], []
    for d in range(2):
        in_specs += [
            pl.BlockSpec((None, M_WIDTH, L), lambda i, j, d=d: (i, T_Q // M_WIDTH, chunk[d](j))),
            pl.BlockSpec((None, L, M_WIDTH), lambda i, j, d=d: (i, chunk[d](j), P_K // M_WIDTH)),
            pl.BlockSpec((None, M_WIDTH, L), lambda i, j, d=d: (i, T_V // M_WIDTH, chunk[d](j))),
            pl.BlockSpec((None, L, LANES), lambda i, j, d=d: (i, chunk[d](j), d))]
        args += [qvo_t, p_main, qvo_t, p_gates]
    in_specs.append(pl.BlockSpec((2, 1, LANES), lambda i, j: (0, 0, 0)))
    args.append(gate_b2)
    st_c = pl.BlockSpec((None, 2, M_HEADS, NV, M_DK), lambda i, j: (i, 0, 0, 0, 0))
    st_m = pl.BlockSpec((None, 2, M_HEADS, 8, LANES), lambda i, j: (i, 0, 0, 0, 0))
    if init is not None:
        in_specs += [st_c, st_m]
        args += list(init)
    out_specs = [pl.BlockSpec((None, M_WIDTH, L), lambda i, j, d=d: (i, 0, chunk[d](j)))
                 for d in range(2)]
    out_shape = [jax.ShapeDtypeStruct((b, M_WIDTH, t), F32)] * 2
    if emit_state:
        out_specs += [st_c, st_m]
        out_shape += [jax.ShapeDtypeStruct((b, 2, M_HEADS, NV, M_DK), F32),
                      jax.ShapeDtypeStruct((b, 2, M_HEADS, 8, LANES), F32)]
    return pl.pallas_call(
        functools.partial(_mlstm_kernel, has_init=init is not None, emit_state=emit_state),
        grid=(b, nc),
        in_specs=in_specs,
        out_specs=out_specs,
        out_shape=out_shape,
        scratch_shapes=[pltpu.VMEM((2, M_HEADS, NV, M_DK), F32),
                        pltpu.VMEM((2, M_HEADS, 8, LANES), F32)],
        compiler_params=_cparams(("parallel", "arbitrary")),
        name="mlstm",
    )(*args)


def _mlaprep_kernel(cq_ref, ckv_ref, kr1_ref, kr2_ref, cos_ref, sin_ref, cost_ref, sint_ref,
                    qg_ref, kvg_ref, wq1_ref, wq2_ref, wk_ref, wv_ref, qt_ref, k_ref, vt_ref):
    cos = cos_ref[...]
    sin = sin_ref[...]
    cq = cq_ref[...].astype(F32)
    cqn = (_rms(cq) * qg_ref[...]).astype(BF16)
    ckv = ckv_ref[...].astype(F32)
    ckvn = (_rms(ckv) * kvg_ref[...]).astype(BF16)
    a_scale = (B_NOPE + B_ROPE) ** -0.5 * LOG2E
    kr = kr1_ref[...].astype(F32) * cos + kr2_ref[...].astype(F32) * sin
    cos_t = cost_ref[...] * a_scale
    sin_t = sint_ref[...] * a_scale
    ones = jnp.ones((V_EXT - B_V, vt_ref.shape[1]), vt_ref.dtype)
    v_all = _dot_nt(wv_ref[...], ckvn).astype(vt_ref.dtype)
    for h in range(B_HEADS):
        vt_ref[h * V_EXT:h * V_EXT + B_V, :] = v_all[h * B_V:(h + 1) * B_V]
        vt_ref[h * V_EXT + B_V:(h + 1) * V_EXT, :] = ones
    q1 = _dot_nt(wq1_ref[...], cqn)
    q2 = _dot_nt(wq2_ref[...], cqn)
    kn = _dot(ckvn, wk_ref[...])
    for h in range(B_HEADS):
        sl = slice(h * LANES, (h + 1) * LANES)
        qt_ref[sl, :] = (q1[sl] * cos_t + q2[sl] * sin_t).astype(qt_ref.dtype)
        k_ref[:, sl] = (kn[:, sl] + kr).astype(k_ref.dtype)


def _mlaprep(p_main, tables, qg, kvg, wq1, wq2, wk, wv, tm):
    cos, sin, cos_t, sin_t = tables
    b, t, _ = p_main.shape
    tm = min(tm, t)
    hw = B_HEADS * LANES

    def tok(width, cb):
        return pl.BlockSpec((None, tm, width), lambda i, j: (i, j, cb))

    def full(a):
        return pl.BlockSpec(a.shape, lambda i, j: (0,) * a.ndim)

    return pl.pallas_call(
        _mlaprep_kernel,
        grid=(b, t // tm),
        in_specs=[tok(B_Q_RANK, P_CQ // B_Q_RANK), tok(LANES, P_CKV // LANES),
                  tok(LANES, P_KR1 // LANES), tok(LANES, P_KR2 // LANES),
                  pl.BlockSpec((tm, LANES), lambda i, j: (j, 0)),
                  pl.BlockSpec((tm, LANES), lambda i, j: (j, 0)),
                  pl.BlockSpec((LANES, tm), lambda i, j: (0, j)),
                  pl.BlockSpec((LANES, tm), lambda i, j: (0, j)),
                  full(qg), full(kvg), full(wq1), full(wq2), full(wk), full(wv)],
        out_specs=[pl.BlockSpec((None, hw, tm), lambda i, j: (i, 0, j)),
                   pl.BlockSpec((None, tm, hw), lambda i, j: (i, j, 0)),
                   pl.BlockSpec((None, B_HEADS * V_EXT, tm), lambda i, j: (i, 0, j))],
        out_shape=[jax.ShapeDtypeStruct((b, hw, t), BF16),
                   jax.ShapeDtypeStruct((b, t, hw), BF16),
                   jax.ShapeDtypeStruct((b, B_HEADS * V_EXT, t), BF16)],
        compiler_params=_cparams(("parallel", "parallel")),
        name="mlaprep",
    )(p_main, p_main, p_main, p_main, cos, sin, cos_t, sin_t, qg, kvg, wq1, wq2, wk, wv)


def _flash_kernel(*refs, tk, nk, has_ctx):
    if has_ctx:
        q_ref, k_ref, v_ref, kc_ref, vc_ref, o_ref, s_s, p_s = refs
    else:
        q_ref, k_ref, v_ref, o_ref, s_s, p_s = refs
    tq = q_ref.shape[1]
    sls = [slice(h * LANES, (h + 1) * LANES) for h in range(2)]
    vsl = [slice(h * V_EXT, (h + 1) * V_EXT) for h in range(2)]

    def stage_a(h, slot, kb):
        s = _dot(kb, q_ref[sls[h], :])
        s_s[h, slot, 0:kb.shape[0]] = s
        return jnp.max(s, axis=0, keepdims=True)

    def stage_b(h, slot, rows, mx, m):
        m_new = jnp.maximum(m, mx)
        p_s[h, slot, 0:rows] = jnp.exp2(s_s[h, slot, 0:rows] - m_new).astype(BF16)
        return m_new, jnp.exp2(m - m_new)

    def stage_c(h, slot, vb, alpha, acc):
        return alpha * acc + _dot(vb, p_s[h, slot, 0:vb.shape[1]])

    def kblock(j, h):
        return k_ref[pl.ds(pl.multiple_of(j * tk, tk), tk), sls[h]]

    def vblock(j, h):
        return v_ref[vsl[h], pl.ds(pl.multiple_of(j * tk, LANES), tk)]

    def trip(slot, state, k_new, v_old):
        mx_new = [stage_a(h, slot, k_new(h)) for h in range(2)]
        mb = [stage_b(h, 1 - slot, tk, state[h][0], state[h][1]) for h in range(2)]
        acc = [stage_c(h, slot, v_old(h), state[h][2], state[h][3]) for h in range(2)]
        return tuple((mx_new[h], mb[h][0], mb[h][1], acc[h]) for h in range(2))

    zrow = jnp.zeros((1, tq), F32)
    neg = jnp.full((1, tq), NEG, F32)
    zacc = jnp.zeros((V_EXT, tq), F32)
    state = []
    for h in range(2):
        mx = stage_a(h, 0, k_ref[0:tk, sls[h]])
        if nk > 1:
            mx1 = stage_a(h, 1, k_ref[tk:2 * tk, sls[h]])
            m, alpha = stage_b(h, 0, tk, mx, neg)
            state.append((mx1, m, alpha, zacc))
        else:
            state.append((mx, neg, zrow, zacc))
    state = tuple(state)
    if nk > 2:
        assert nk % 2 == 0

        def body(i, state):
            for par in range(2):
                j = 2 * i + par
                state = trip(par, state, functools.partial(kblock, j),
                             functools.partial(vblock, j - 2))
            return state

        state = lax.fori_loop(1, nk // 2, body, state)
    last = (nk - 1) % 2
    tail_rows = tk
    if has_ctx:
        assert nk > 1 and kc_ref.shape[0] <= tk
        state = trip(1 - last, state, lambda h: kc_ref[:, sls[h]],
                     lambda h: v_ref[vsl[h], (nk - 2) * tk:(nk - 1) * tk])
        tail_rows = kc_ref.shape[0]
    outs = []
    for h in range(2):
        mx, m, alpha, acc = state[h]
        if has_ctx:
            acc = stage_c(h, last, v_ref[vsl[h], (nk - 1) * tk:nk * tk], alpha, acc)
            m, alpha = stage_b(h, 1 - last, tail_rows, mx, m)
            acc = stage_c(h, 1 - last, vc_ref[vsl[h], :], alpha, acc)
        else:
            if nk > 1:
                acc = stage_c(h, 1 - last, v_ref[vsl[h], (nk - 2) * tk:(nk - 1) * tk], alpha, acc)
            m, alpha = stage_b(h, last, tail_rows, mx, m)
            acc = stage_c(h, last, v_ref[vsl[h], (nk - 1) * tk:nk * tk], alpha, acc)
        outs.append(acc[0:B_V] / acc[B_V:B_V + 1])
    o_ref[...] = jnp.concatenate(outs, axis=0).T.astype(o_ref.dtype)


def _flash(q_t, k, v_t, kc, vc_t, tq, tk):
    b, _, t = q_t.shape
    tkeys = k.shape[1]
    tq = min(tq, t)
    tk = min(tk, tkeys)
    assert tkeys % tk == 0
    has_ctx = kc is not None
    in_specs = [pl.BlockSpec((None, 2 * LANES, tq), lambda i, hp, j: (i, hp, j)),
                pl.BlockSpec((None, tkeys, 2 * LANES), lambda i, hp, j: (i, 0, hp)),
                pl.BlockSpec((None, 2 * V_EXT, tkeys), lambda i, hp, j: (i, hp, 0))]
    args = [q_t, k, v_t]
    if has_ctx:
        tc = kc.shape[1]
        in_specs += [pl.BlockSpec((None, tc, 2 * LANES), lambda i, hp, j: (i, 0, hp)),
                     pl.BlockSpec((None, 2 * V_EXT, tc), lambda i, hp, j: (i, hp, 0))]
        args += [kc, vc_t]
    return pl.pallas_call(
        functools.partial(_flash_kernel, tk=tk, nk=tkeys // tk, has_ctx=has_ctx),
        grid=(b, B_HEADS // 2, t // tq),
        in_specs=in_specs,
        out_specs=pl.BlockSpec((None, tq, LANES), lambda i, hp, j: (i, j, hp)),
        out_shape=jax.ShapeDtypeStruct((b, t, B_WIDTH), BF16),
        scratch_shapes=[pltpu.VMEM((2, 2, tk, tq), F32), pltpu.VMEM((2, 2, tk, tq), BF16)],
        compiler_params=_cparams(("parallel", "parallel", "arbitrary")),
        name="flash",
    )(*args)


NA_PAIR_ROWS = NA_KH + 2
NA_PAIR_KEYS = NA_PAIR_ROWS * GRID_W
NA_VARIANTS = 5


def _natten_pair_window(r, rows):
    return jnp.clip(r - NA_KH // 2, 0, rows - NA_PAIR_ROWS)


def _natten_kernel(qt_ref, k_ref, vt_ref, kc_ref, vct_ref, bias_ref, o_ref, *, rb, rows):
    blk = pl.program_id(2)
    tq = rb * GRID_W
    pq = 2 * GRID_W
    npair = rb // 2
    q_t = qt_ref[...]
    feat = lax.broadcasted_iota(jnp.int32, q_t.shape, 0)
    zero = jnp.zeros_like(q_t)
    qh = [jnp.where(feat < C_HEAD_DIM, q_t, zero), jnp.where(feat >= C_HEAD_DIM, q_t, zero)]
    q2 = [jnp.concatenate([qh[0][:, i * pq:(i + 1) * pq], qh[1][:, i * pq:(i + 1) * pq]], axis=1)
          for i in range(npair)]
    s_c = _dot(kc_ref[...], jnp.concatenate(q2, axis=1))
    m_c = jnp.max(s_c, axis=0, keepdims=True)

    def scores(i):
        r = blk * rb + 2 * i
        ws = _natten_pair_window(r, rows)
        koff = pl.multiple_of(ws * GRID_W, LANES)
        s_w = _dot(k_ref[pl.ds(koff, NA_PAIR_KEYS), :], q2[i])
        var = (r - ws) // 2
        return koff, s_w + jnp.concatenate([bias_ref[0, var], bias_ref[1, var]], axis=1)

    def softmax(i, s_w):
        csl = slice(i * 2 * pq, (i + 1) * 2 * pq)
        m = jnp.maximum(jnp.max(s_w, axis=0, keepdims=True), m_c[:, csl])
        p_w = jnp.exp2(s_w - m)
        p_ci = jnp.exp2(s_c[:, csl] - m)
        l = jnp.sum(p_w, axis=0, keepdims=True) + jnp.sum(p_ci, axis=0, keepdims=True)
        return p_w.astype(BF16), p_ci.astype(BF16), l

    o_w, p_c, l_all = [], [], []
    sc = {0: scores(0)}
    if npair > 1:
        sc[1] = scores(1)
    sm = {0: softmax(0, sc[0][1])}
    for i in range(npair):
        if i + 2 < npair:
            sc[i + 2] = scores(i + 2)
        if i + 1 < npair:
            sm[i + 1] = softmax(i + 1, sc[i + 1][1])
        p_w, p_ci, l = sm.pop(i)
        koff = sc.pop(i)[0]
        p_c.append(p_ci)
        l_all.append(l)
        o_w.append(_dot(vt_ref[:, pl.ds(koff, NA_PAIR_KEYS)], p_w))
    o2 = ((jnp.concatenate(o_w, axis=1) + _dot(vct_ref[...], jnp.concatenate(p_c, axis=1)))
          / jnp.concatenate(l_all, axis=1))
    top = lax.broadcasted_iota(jnp.int32, (LANES, pq), 0) < C_HEAD_DIM
    out_t = jnp.concatenate(
        [jnp.where(top, o2[:, i * 2 * pq:i * 2 * pq + pq], o2[:, i * 2 * pq + pq:(i + 1) * 2 * pq])
         for i in range(npair)], axis=1)
    o_ref[...] = out_t.T.astype(o_ref.dtype)


def _natten(qv_t, k, kc, vc_t, bias_t, rb):
    b, t, _ = k.shape
    tc = kc.shape[1]
    rows = t // GRID_W
    rb = min(rb, rows)
    assert rows % 2 == 0 and rows >= NA_PAIR_ROWS + 4 and rb % 2 == 0
    nhp = C_HEADS // 2
    return pl.pallas_call(
        functools.partial(_natten_kernel, rb=rb, rows=rows),
        grid=(b, nhp, rows // rb),
        in_specs=[pl.BlockSpec((None, LANES, rb * GRID_W), lambda i, hp, j: (i, hp, j)),
                  pl.BlockSpec((None, t, LANES), lambda i, hp, j: (i, 0, hp)),
                  pl.BlockSpec((None, LANES, t), lambda i, hp, j: (i, nhp + hp, 0)),
                  pl.BlockSpec((None, tc, LANES), lambda i, hp, j: (i, 0, hp)),
                  pl.BlockSpec((None, LANES, tc), lambda i, hp, j: (i, hp, 0)),
                  pl.BlockSpec((None,) + bias_t.shape[1:], lambda i, hp, j: (hp, 0, 0, 0, 0))],
        out_specs=pl.BlockSpec((None, rb * GRID_W, LANES), lambda i, hp, j: (i, j, hp)),
        out_shape=jax.ShapeDtypeStruct((b, t, C_HEADS * C_HEAD_DIM), BF16),
        compiler_params=_cparams(("parallel", "parallel", "arbitrary")),
        name="natten",
    )(qv_t, k, qv_t, kc, vc_t, bias_t)


def _natten_bias_table(rel_bias):
    kcol = np.arange(GRID_W)[:, None]
    qc = np.arange(GRID_W)[None, :]
    cs = np.clip(qc - NA_KW // 2, 0, GRID_W - NA_KW)
    valid = (kcol >= cs) & (kcol < cs + NA_KW)
    pad = GRID_W - NA_KW
    rev = jnp.pad(rel_bias.astype(F32) * LOG2E, ((0, 0), (0, 0), (pad, pad)))[:, :, ::-1]
    toe = jnp.stack([rev[:, :, GRID_W - 1 - k:2 * GRID_W - 1 - k] for k in range(GRID_W)], axis=2)
    toe = jnp.where(valid[None, None], toe, NEG)
    nri = 2 * NA_KH - 1
    neg = jnp.full_like(toe[:, :1], NEG)
    up = jnp.concatenate([toe, neg], axis=1)
    down = jnp.concatenate([neg, toe], axis=1)
    negs = jnp.full_like(up, NEG)
    both = jnp.concatenate([up, down], axis=-1)
    left = jnp.concatenate([up, negs], axis=-1)
    right = jnp.concatenate([negs, down], axis=-1)
    none = jnp.full_like(both[:, 0], NEG)
    own = ((0, 0), (0, 0), (0, 1), (2, 2), (2, 2))
    variants = []
    for v in range(NA_VARIANTS):
        key_rows = []
        for x in range(NA_PAIR_ROWS):
            ri0 = x - 2 * v + NA_KH - 1
            has = [own[v][j] <= x < own[v][j] + NA_KH for j in range(2)]
            assert not any(has) or 0 <= ri0 <= nri
            src = both if all(has) else left if has[0] else right if has[1] else None
            key_rows.append(none if src is None else src[:, ri0])
        variants.append(jnp.concatenate(key_rows, axis=1))
    tab = jnp.stack(variants, axis=1)
    return tab.reshape(C_HEADS // 2, 2, NA_VARIANTS, NA_PAIR_KEYS, 2 * GRID_W)


def _mlp_tail(x1, g2, sh2, sc2, n2g, w1_ref, w2_ref):
    h = (_rms(x1) * n2g * (1.0 + sc2) + sh2).astype(BF16)
    acc = jnp.zeros_like(x1)
    chunk = 1024
    for c0 in range(0, D_FF, chunk):
        a = jnp.maximum(_dot(h, w1_ref[:, c0:c0 + chunk]), 0.0)
        acc = acc + _dot((a * a).astype(BF16), w2_ref[c0:c0 + chunk, :])
    return x1 + g2 * acc


def _post_ab_kernel(x_ref, hf_ref, hb_ref, og_ref, bl_ref, mg_ref, g1_ref, sh2_ref, sc2_ref,
                    g2_ref, n2g_ref, wo_ref, w1_ref, w2_ref, o_ref):
    hm = hf_ref[...] + hb_ref[...]
    og = jax.nn.sigmoid(og_ref[...].astype(F32))
    mg = mg_ref[...]
    y = _dot(bl_ref[...], wo_ref[M_WIDTH:, :])
    for h in range(M_HEADS):
        sl = slice(h * M_DV, (h + 1) * M_DV)
        xh = hm[sl]
        inv = lax.rsqrt(jnp.mean(xh * xh, axis=0, keepdims=True) + EPS)
        hn = (xh * inv * mg[sl] * og[sl]).astype(BF16)
        y = y + lax.dot_general(hn, wo_ref[sl, :], (((0,), (0,)), ((), ())),
                                preferred_element_type=F32)
    x1 = x_ref[...] + g1_ref[...] * y
    o_ref[...] = _mlp_tail(x1, g2_ref[...], sh2_ref[...], sc2_ref[...], n2g_ref[...],
                           w1_ref, w2_ref)


def _post_c_kernel(x_ref, y_ref, g1_ref, sh2_ref, sc2_ref, g2_ref, n2g_ref, fg_ref,
                   wo_ref, w1_ref, w2_ref, o_ref):
    x1 = x_ref[...] + g1_ref[...] * _dot(y_ref[...], wo_ref[...])
    x2 = _mlp_tail(x1, g2_ref[...], sh2_ref[...], sc2_ref[...], n2g_ref[...], w1_ref, w2_ref)
    o_ref[...] = _rms(x2) * fg_ref[...]


def _resident(a):
    return pl.BlockSpec(a.shape, lambda i, j: (0,) * a.ndim, pipeline_mode=pl.Buffered(1))


def _post_ab(x, h_dirs, qvo_t, b_lat, mg, mods, n2g, wo, w1, w2, tm):
    b, t, d = x.shape
    tm = min(tm, t)
    g1, sh2, sc2, g2 = mods
    bm = g1.shape[0]
    mod_map = (lambda i, j: (i, 0, 0)) if bm == b else (lambda i, j: (0, 0, 0))
    mod_spec = pl.BlockSpec((None, 1, d), mod_map)
    tok = lambda w, cb: pl.BlockSpec((None, tm, w), lambda i, j: (i, j, cb))
    return pl.pallas_call(
        _post_ab_kernel,
        grid=(b, t // tm),
        in_specs=[tok(d, 0),
                  pl.BlockSpec((None, M_WIDTH, tm), lambda i, j: (i, 0, j)),
                  pl.BlockSpec((None, M_WIDTH, tm), lambda i, j: (i, 0, j)),
                  pl.BlockSpec((None, M_WIDTH, tm), lambda i, j: (i, T_O // M_WIDTH, j)),
                  tok(B_WIDTH, 0),
                  pl.BlockSpec((M_WIDTH, 1), lambda i, j: (0, 0)),
                  mod_spec, mod_spec, mod_spec, mod_spec,
                  pl.BlockSpec((1, d), lambda i, j: (0, 0)),
                  _resident(wo), _resident(w1), _resident(w2)],
        out_specs=tok(d, 0),
        out_shape=jax.ShapeDtypeStruct((b, t, d), F32),
        compiler_params=_cparams(("parallel", "parallel")),
        name="post_ab",
    )(x, h_dirs[0], h_dirs[1], qvo_t, b_lat, mg, g1, sh2, sc2, g2, n2g.reshape(1, d), wo, w1, w2)


def _post_c(x, y, mods, n2g, fg, wo, w1, w2, tm):
    b, t, d = x.shape
    tm = min(tm, t)
    g1, sh2, sc2, g2 = mods
    mod_spec = pl.BlockSpec((None, 1, d), lambda i, j: (i, 0, 0))
    tok = pl.BlockSpec((None, tm, d), lambda i, j: (i, j, 0))
    vec = pl.BlockSpec((1, d), lambda i, j: (0, 0))
    return pl.pallas_call(
        _post_c_kernel,
        grid=(b, t // tm),
        in_specs=[tok, tok, mod_spec, mod_spec, mod_spec, mod_spec, vec, vec,
                  _resident(wo), _resident(w1), _resident(w2)],
        out_specs=tok,
        out_shape=jax.ShapeDtypeStruct((b, t, d), F32),
        compiler_params=_cparams(("parallel", "parallel")),
        name="post_c",
    )(x, y, g1, sh2, sc2, g2, n2g.reshape(1, d), fg.reshape(1, d), wo, w1, w2)


def _rope_tables(t):
    pos = np.arange(t)
    row = (pos // GRID_W).astype(np.float32)
    colp = (pos % GRID_W).astype(np.float32)
    n_f = B_ROPE // 4
    freqs = (ROPE_BASE ** (-np.arange(n_f, dtype=np.float32) / n_f)).astype(np.float32)
    ang = np.concatenate([row[:, None] * freqs, colp[:, None] * freqs], axis=-1)
    cos = np.zeros((t, LANES), np.float32)
    sin = np.zeros((t, LANES), np.float32)
    cos[:, :B_NOPE] = 1.0
    cos[:, B_NOPE:B_NOPE + 16] = np.cos(ang)
    cos[:, B_NOPE + 16:B_NOPE + 32] = np.cos(ang)
    sin[:, B_NOPE:B_NOPE + 16] = np.sin(ang)
    sin[:, B_NOPE + 16:B_NOPE + 32] = np.sin(ang)
    return tuple(jnp.asarray(a) for a in (cos, sin, cos.T.copy(), sin.T.copy()))


def _identity_tables(t):
    cos = np.zeros((t, LANES), np.float32)
    cos[:, :B_NOPE + B_ROPE] = 1.0
    sin = np.zeros((t, LANES), np.float32)
    return tuple(jnp.asarray(a) for a in (cos, sin, cos.T.copy(), sin.T.copy()))


def _layer0_weights(w_in, gate_b, w_uq, w_ukv):
    d = w_in.shape[0]
    half = B_ROPE // 2
    o_g = 4 * M_WIDTH
    o_cq = o_g + 4 * M_HEADS
    o_ckv = o_cq + B_Q_RANK
    o_kr = o_ckv + B_KV_RANK
    kr = w_in[:, o_kr:o_kr + B_ROPE]
    z = lambda n: jnp.zeros((d, n), w_in.dtype)
    kr1 = jnp.concatenate([z(B_NOPE), kr, z(LANES - B_NOPE - B_ROPE)], axis=1)
    kr2 = jnp.concatenate([z(B_NOPE), -kr[:, half:], kr[:, :half], z(LANES - B_NOPE - B_ROPE)], axis=1)
    wg = w_in[:, o_g:o_g + 4 * M_HEADS]
    gpad = z(LANES - 2 * M_HEADS)
    mq, mk, mv, mo = (w_in[:, i * M_WIDTH:(i + 1) * M_WIDTH] for i in range(4))
    w0 = jnp.concatenate([mk, w_in[:, o_cq:o_kr], kr1, kr2,
                          wg[:, :2 * M_HEADS], gpad, wg[:, 2 * M_HEADS:], gpad], axis=1).astype(BF16)
    w0_t = jnp.concatenate([mq, mv, mo], axis=1).T.astype(BF16)
    gb = jnp.zeros((2, 1, LANES), F32)
    gb = gb.at[0, 0, :2 * M_HEADS].set(gate_b[:2 * M_HEADS]).at[1, 0, :2 * M_HEADS].set(gate_b[2 * M_HEADS:])

    qr = B_Q_RANK
    wq = w_uq.reshape(qr, B_HEADS, B_NOPE + B_ROPE)
    nope, ra, rb = wq[..., :B_NOPE], wq[..., B_NOPE:B_NOPE + half], wq[..., B_NOPE + half:]
    zq = jnp.zeros((qr, B_HEADS, LANES - B_NOPE - B_ROPE), w_uq.dtype)
    wq1 = jnp.concatenate([nope, ra, rb, zq], axis=-1).reshape(qr, B_HEADS * LANES).astype(BF16)
    wq2 = jnp.concatenate([jnp.zeros_like(nope), -rb, ra, zq], axis=-1).reshape(qr, B_HEADS * LANES).astype(BF16)
    wkv = w_ukv.reshape(B_KV_RANK, B_HEADS, B_NOPE + B_V)
    wk = jnp.concatenate([wkv[..., :B_NOPE], jnp.zeros((B_KV_RANK, B_HEADS, LANES - B_NOPE), w_ukv.dtype)],
                         axis=-1).reshape(B_KV_RANK, B_HEADS * LANES).astype(BF16)
    wv = wkv[..., B_NOPE:].reshape(B_KV_RANK, B_WIDTH).astype(BF16)
    return w0, w0_t, gb, wq1.T, wq2.T, wk, wv.T


def _split_mod(mod_rows):
    return [m[:, None, :] for m in jnp.split(mod_rows, 6, axis=-1)]


def kernel(x, c, ctx, c_ctx, ada_w, ada_b, norm1_g, norm2_g, mlp_w1, mlp_w2, ab_w_in, ab_gate_b,
           ab_m_norm_g, ab_q_norm_g, ab_kv_norm_g, ab_w_uq, ab_w_ukv, ab_w_out, na_w_in,
           na_rel_bias, na_w_out, final_norm_g):
    b, t, d = x.shape
    tc = ctx.shape[1]
    c_rows = jnp.concatenate([c, c_ctx[None, :], jnp.zeros((8 - b - 1, d), F32)], axis=0)
    mod = _ada(c_rows, ada_w, ada_b)

    sh1, sc1, g1, sh2, sc2, g2 = _split_mod(mod[0, :b])
    csh1, csc1, cg1, csh2, csc2, cg2 = _split_mod(mod[0, b:b + 1])
    w0, w0_t, gb, wq1, wq2, wk, wv = _layer0_weights(ab_w_in[0], ab_gate_b[0], ab_w_uq[0],
                                                      ab_w_ukv[0])
    widths, dtypes = (P_MAIN, P_GATES), (BF16, F32)
    p_main, p_gates, qvo_t = _modproj(x, sh1, sc1, norm1_g[0], w0, widths, dtypes, tm=TOKEN_TILE, w_t=w0_t)
    pc_main, pc_gates, qvo_ct = _modproj(ctx, csh1, csc1, norm1_g[0], w0, widths, dtypes, tm=TOKEN_TILE,
                                         w_t=w0_t)

    hc_f, hc_b, st_c, st_m = _mlstm(pc_main, pc_gates, qvo_ct, gb, None, True)
    h_dirs = _mlstm(p_main, p_gates, qvo_t, gb, (st_c, st_m), False)
    hc_dirs = (hc_f, hc_b)

    qg = ab_q_norm_g[0].reshape(1, -1)
    kvg = ab_kv_norm_g[0].reshape(1, -1)
    q_l, k_l, v_l = _mlaprep(p_main, _rope_tables(t), qg, kvg, wq1, wq2, wk, wv, tm=TOKEN_TILE)
    q_c, k_c, v_c = _mlaprep(pc_main, _identity_tables(tc), qg, kvg, wq1, wq2, wk, wv, tm=TOKEN_TILE)
    b_lat = _flash(q_l, k_l, v_l, k_c, v_c, tq=FLASH_TQ, tk=FLASH_TK)
    b_ctx = _flash(q_c, k_c, v_c, None, None, tq=FLASH_TQ, tk=FLASH_TK)

    wo = ab_w_out[0].astype(BF16)
    w1 = mlp_w1[0].astype(BF16)
    w2 = mlp_w2[0].astype(BF16)
    mg = ab_m_norm_g[0].reshape(M_WIDTH, 1)
    x = _post_ab(x, h_dirs, qvo_t, b_lat, mg, (g1, sh2, sc2, g2), norm2_g[0], wo, w1, w2, tm=TOKEN_TILE)
    ctx = _post_ab(ctx, hc_dirs, qvo_ct, b_ctx, mg, (cg1, csh2, csc2, cg2), norm2_g[0],
                   wo, w1, w2, tm=TOKEN_TILE)

    sh1, sc1, g1, sh2, sc2, g2 = _split_mod(mod[1, :b])
    csh1, csc1 = _split_mod(mod[1, b:b + 1])[:2]
    cw = C_HEADS * C_HEAD_DIM
    w_q = na_w_in[0][:, :cw] * (C_HEAD_DIM ** -0.5 * LOG2E)
    w_k = na_w_in[0][:, cw:2 * cw].astype(BF16)
    w_qv_t = jnp.concatenate([w_q, na_w_in[0][:, 2 * cw:]], axis=1).T.astype(BF16)
    k_n, qv_t = _modproj(x, sh1, sc1, norm1_g[1], w_k, (cw,), (BF16,), tm=TOKEN_TILE, w_t=w_qv_t)
    kc_n, vc_t = _modproj(ctx, csh1, csc1, norm1_g[1], w_k, (cw,), (BF16,), tm=TOKEN_TILE,
                          w_t=w_qv_t[cw:])
    bias_t = _natten_bias_table(na_rel_bias[0])
    y = _natten(qv_t, k_n, kc_n, vc_t, bias_t, rb=NATTEN_ROWS)
    return _post_c(x, y, (g1, sh2, sc2, g2), norm2_g[1], final_norm_g,
                   na_w_out[0].astype(BF16), mlp_w1[1].astype(BF16), mlp_w2[1].astype(BF16), tm=TOKEN_TILE)
```
